```python
import math
import jax, jax.numpy as jnp
from jax import lax
import numpy as np

D_MODEL = 4096
BATCH = 8
SEQ = 2048
DEPTH = 2
DEC_BATCH = 32
DEC_SEQ = 32
PAST_LEN = 2048

CHUNK = 64
A_HEADS = 16
A_DK = 128
A_DV = 128
B_HEADS = 8
B_DK = 128
B_DV = 256
C_HEADS = 16
C_DK = 128
C_DV = 128
CONV_W = 4
D_FF = 14336
N_EXPERTS = 8
TOP_K = 2
N_DENSE = (DEPTH + 1) // 2
N_MOE = DEPTH // 2

A_QK = A_HEADS * A_DK
A_V = A_HEADS * A_DV
B_QK = B_HEADS * B_DK
B_V = B_HEADS * B_DV
C_QK = C_HEADS * C_DK
C_V = C_HEADS * C_DV
C_CONV = 2 * C_QK + C_V
COL_SIZES = (A_QK, A_QK, A_V, A_V, B_QK, B_QK, B_V, B_V, C_CONV, C_HEADS, C_HEADS, C_V, D_MODEL, D_MODEL, D_MODEL)
PROJ_COLS = sum(COL_SIZES)
ROPE_BASE = 10000.0
LN_EPS = 1e-5
RMS_EPS = 1e-6
DEEPNORM_ALPHA = (2 * DEPTH) ** 0.25
DEEPNORM_BETA = (8 * DEPTH) ** -0.25

kernel_name = 'hybrid_stream_hgrn2_retnet_gdn_step'

F32 = jnp.float32


def _layer_norm(x, g, b):
    xf = x.astype(F32)
    mu = jnp.mean(xf, -1, keepdims=True)
    var = jnp.mean(jnp.square(xf - mu), -1, keepdims=True)
    y = (xf - mu) * lax.rsqrt(var + LN_EPS) * g.astype(F32) + b.astype(F32)
    return y.astype(x.dtype)


def _rms_norm(x, g):
    xf = x.astype(F32)
    y = xf * lax.rsqrt(jnp.mean(jnp.square(xf), -1, keepdims=True) + RMS_EPS) * g.astype(F32)
    return y.astype(x.dtype)


def _group_norm(x):
    xf = x.astype(F32)
    mu = jnp.mean(xf, -1, keepdims=True)
    var = jnp.mean(jnp.square(xf - mu), -1, keepdims=True)
    return ((xf - mu) * lax.rsqrt(var + LN_EPS)).astype(x.dtype)


def _l2norm(x):
    xf = x.astype(F32)
    return (xf * lax.rsqrt(jnp.sum(xf * xf, -1, keepdims=True) + RMS_EPS)).astype(x.dtype)


def _rope(x, pos):
    half = x.shape[-1] // 2
    inv_freq = ROPE_BASE ** (-jnp.arange(half, dtype=F32) / half)
    ang = pos.astype(F32)[:, None] * inv_freq[None, :]
    cos = jnp.cos(ang)[None, :, None, :]
    sin = jnp.sin(ang)[None, :, None, :]
    xf = x.astype(F32)
    x1, x2 = xf[..., :half], xf[..., half:]
    return jnp.concatenate([x1 * cos - x2 * sin, x1 * sin + x2 * cos], -1).astype(x.dtype)


def _to_chunks(x, c):
    b, t, h, d = x.shape
    return x.reshape(b, t // c, c, h, d).transpose(1, 0, 3, 2, 4)


def _from_chunks(x):
    nc, b, h, c, d = x.shape
    return x.transpose(1, 0, 3, 2, 4).reshape(b, nc * c, h, d)


def _chunk_gla(q, k, v, log_f, s0, c):
    idx = jnp.arange(c)
    causal = (idx[:, None] >= idx[None, :])[:, :, None]

    def step(s, inp):
        qc, kc, vc, gc = inp
        bcum = jnp.cumsum(gc.astype(F32), axis=2)
        diff = bcum[:, :, :, None, :] - bcum[:, :, None, :, :]
        dec = jnp.exp(jnp.where(causal, diff, -jnp.inf)).astype(qc.dtype)
        scores = jnp.einsum('bhtsd,bhsd->bhts', qc[:, :, :, None, :] * dec, kc)
        o = (jnp.einsum('bhts,bhsv->bhtv', scores, vc)
             + jnp.einsum('bhtd,bhdv->bhtv', qc * jnp.exp(bcum).astype(qc.dtype), s))
        blast = bcum[:, :, -1:, :]
        s_new = (jnp.exp(blast[:, :, 0, :]).astype(s.dtype)[..., None] * s
                 + jnp.einsum('bhsd,bhsv->bhdv', kc * jnp.exp(blast - bcum).astype(kc.dtype), vc))
        return s_new.astype(s.dtype), o

    xs = (_to_chunks(q, c), _to_chunks(k, c), _to_chunks(v, c), _to_chunks(log_f, c))
    s_fin, o = lax.scan(step, s0, xs)
    return _from_chunks(o), s_fin


def _chunk_retention(q, k, v, log_gamma, s0, c):
    idx = jnp.arange(c, dtype=F32)
    rel = idx[:, None] - idx[None, :]
    lg = log_gamma[:, None, None]
    dt = q.dtype
    intra = jnp.where(rel >= 0, jnp.exp(jnp.maximum(rel, 0.0) * lg), 0.0).astype(dt)
    inner = jnp.exp((idx + 1.0)[None, :] * log_gamma[:, None]).astype(dt)
    tail = jnp.exp((c - 1.0 - idx)[None, :] * log_gamma[:, None]).astype(dt)
    whole = jnp.exp(c * log_gamma).astype(dt)

    def step(s, inp):
        qc, kc, vc = inp
        scores = jnp.einsum('bhtd,bhsd->bhts', qc, kc) * intra
        o = (jnp.einsum('bhts,bhsv->bhtv', scores, vc)
             + jnp.einsum('bhtd,bhdv->bhtv', qc, s) * inner[:, :, None])
        s_new = whole[:, None, None] * s + jnp.einsum('bhsd,bhsv->bhdv', kc * tail[:, :, None], vc)
        return s_new.astype(s.dtype), o

    xs = (_to_chunks(q, c), _to_chunks(k, c), _to_chunks(v, c))
    s_fin, o = lax.scan(step, s0, xs)
    return _from_chunks(o), s_fin


def _chunk_gated_delta(q, k, v, g, beta, s0, c):
    idx = jnp.arange(c)
    incl = idx[:, None] >= idx[None, :]
    strict = idx[:, None] > idx[None, :]
    eye = jnp.eye(c, dtype=F32)
    dv = v.shape[-1]

    def step(s, inp):
        qc, kc, vc, gc, bc = inp
        dt = qc.dtype
        bcum = jnp.cumsum(gc[..., 0].astype(F32), axis=-1)
        diff = bcum[..., :, None] - bcum[..., None, :]
        lmask = jnp.where(incl, jnp.exp(jnp.where(incl, diff, 0.0)), 0.0)
        kb = kc * bc
        a_low = jnp.where(strict, jnp.einsum('bhtd,bhsd->bhts', kb, kc).astype(F32) * lmask, 0.0)
        rhs = jnp.concatenate([(vc * bc).astype(F32), kb.astype(F32) * jnp.exp(bcum)[..., None]], -1)
        sol = lax.linalg.triangular_solve(eye + a_low, rhs, left_side=True, lower=True, unit_diagonal=True)
        u = sol[..., :dv].astype(dt)
        w = sol[..., dv:].astype(dt)
        v_new = u - jnp.einsum('bhtd,bhdv->bhtv', w, s)
        attn = (jnp.einsum('bhtd,bhsd->bhts', qc, kc).astype(F32) * lmask).astype(dt)
        o = (jnp.einsum('bhts,bhsv->bhtv', attn, v_new)
             + jnp.einsum('bhtd,bhdv->bhtv', qc * jnp.exp(bcum)[..., None].astype(dt), s))
        blast = bcum[..., -1:]
        s_new = (jnp.exp(blast)[..., None].astype(dt) * s
                 + jnp.einsum('bhsd,bhsv->bhdv', kc * jnp.exp(blast - bcum)[..., None].astype(dt), v_new))
        return s_new.astype(s.dtype), o

    xs = (_to_chunks(q, c), _to_chunks(k, c), _to_chunks(v, c), _to_chunks(g, c), _to_chunks(beta, c))
    s_fin, o = lax.scan(step, s0, xs)
    return _from_chunks(o), s_fin


def _mixer(h, s_a, s_b, s_c, s_conv, pos0, lb, w_in, norm_a_w, conv_w, a_log, dt_bias, norm_c_w,
           w_br_a, w_br_b, w_br_c, w_out):
    bsz, t, _ = h.shape
    c = math.gcd(t, CHUNK)
    dt = h.dtype
    cuts = [int(v) for v in np.cumsum(COL_SIZES)[:-1]]
    (a_q, a_f, a_i, a_g, b_q, b_k, b_v, b_g, c_qkv, c_a, c_b, c_g,
     gate_a, gate_b, gate_c) = jnp.split(h @ w_in, cuts, axis=-1)

    def heads(z, n):
        return z.reshape(bsz, t, n, -1)

    lbh = jnp.maximum(lb, 0.0).reshape(A_HEADS, A_DK)
    zf = heads(a_f, A_HEADS).astype(F32)
    log_f = jnp.logaddexp(jnp.log(lbh), jnp.log1p(-lbh) + jax.nn.log_sigmoid(zf))
    k_a = ((1.0 - lbh) * jax.nn.sigmoid(-zf)).astype(dt)
    o_a, s_a_new = _chunk_gla(jax.nn.silu(heads(a_q, A_HEADS)), k_a, heads(a_i, A_HEADS), log_f, s_a, c)
    o_a = _rms_norm(o_a, norm_a_w) * jax.nn.silu(heads(a_g, A_HEADS))

    pos = pos0 + jnp.arange(t)
    q_b = _rope(heads(b_q, B_HEADS), pos)
    k_b = _rope(heads(b_k, B_HEADS), pos) * (B_DK ** -0.5)
    log_gamma = jnp.log1p(-jnp.exp2(-5.0 - jnp.arange(B_HEADS, dtype=F32)))
    o_b, s_b_new = _chunk_retention(q_b, k_b, heads(b_v, B_HEADS), log_gamma, s_b, c)
    o_b = _group_norm(o_b) * jax.nn.silu(heads(b_g, B_HEADS))

    conv_in = jnp.concatenate([s_conv.astype(dt), c_qkv], axis=1)
    s_conv_new = conv_in[:, -(CONV_W - 1):]
    qkv = lax.conv_general_dilated(conv_in, conv_w[:, None, :].astype(dt), window_strides=(1,), padding='VALID',
                                   dimension_numbers=('NWC', 'WIO', 'NWC'), feature_group_count=C_CONV)
    qkv = jax.nn.silu(qkv)
    q_c, k_c, v_c = jnp.split(qkv, [C_QK, 2 * C_QK], axis=-1)
    q_c = _l2norm(heads(q_c, C_HEADS)) * (C_DK ** -0.5)
    k_c = _l2norm(heads(k_c, C_HEADS))
    beta = jax.nn.sigmoid(c_b)[..., None]
    g = (-jnp.exp(a_log.astype(F32)) * jax.nn.softplus(c_a.astype(F32) + dt_bias.astype(F32)))[..., None]
    o_c, s_c_new = _chunk_gated_delta(q_c, k_c, heads(v_c, C_HEADS), g, beta, s_c, c)
    o_c = _rms_norm(o_c, norm_c_w) * jax.nn.silu(heads(c_g, C_HEADS))

    merged = (jax.nn.sigmoid(gate_a) * (o_a.reshape(bsz, t, A_V) @ w_br_a)
              + jax.nn.sigmoid(gate_b) * (o_b.reshape(bsz, t, B_V) @ w_br_b)
              + jax.nn.sigmoid(gate_c) * (o_c.reshape(bsz, t, C_V) @ w_br_c))
    return merged @ w_out, s_a_new, s_b_new, s_c_new, s_conv_new


def _swiglu(x, w1, w3, w2):
    return (jax.nn.silu(x @ w1) * (x @ w3)) @ w2


def _moe(x, w_router, w1, w3, w2):
    logits = (x @ w_router).astype(F32)
    top_val, top_idx = lax.top_k(logits, TOP_K)
    top_w = jax.nn.softmax(top_val, axis=-1)
    gate = jnp.sum(jax.nn.one_hot(top_idx, N_EXPERTS, dtype=F32) * top_w[..., None], axis=-2).astype(x.dtype)
    out = jnp.zeros_like(x)
    for e in range(N_EXPERTS):
        hid = jax.nn.silu(x @ w1[e]) * (x @ w3[e]) * gate[..., e:e + 1]
        out = out + hid @ w2[e]
    return out


def _run_trunk(x, st_hgrn, st_ret, st_gdn, st_conv, pos0, params):
    (lower_bounds, w_in, norm_a_w, conv_w, a_log, dt_bias, norm_c_w, w_br_a, w_br_b, w_br_c, w_out,
     ln1_g, ln1_b, ln2_g, ln2_b, ffn_w1, ffn_w3, ffn_w2, router, moe_w1, moe_w3, moe_w2) = params
    p = jax.nn.softmax(lower_bounds.astype(F32), axis=0)
    lb_all = jnp.cumsum(p, axis=0) - p[0:1]
    new_a, new_b, new_c, new_conv = [], [], [], []
    for l in range(DEPTH):
        y, sa, sb, sc, scv = _mixer(x, st_hgrn[l], st_ret[l], st_gdn[l], st_conv[l], pos0, lb_all[l],
                                    w_in[l], norm_a_w[l], conv_w[l], a_log[l], dt_bias[l], norm_c_w[l],
                                    w_br_a[l], w_br_b[l], w_br_c[l], w_out[l])
        x = _layer_norm(DEEPNORM_ALPHA * x + y, ln1_g[l], ln1_b[l])
        if l % 2 == 0:
            f = _swiglu(x, ffn_w1[l // 2], ffn_w3[l // 2], ffn_w2[l // 2])
        else:
            f = _moe(x, router[l // 2], moe_w1[l // 2], moe_w3[l // 2], moe_w2[l // 2])
        x = _layer_norm(DEEPNORM_ALPHA * x + f, ln2_g[l], ln2_b[l])
        new_a.append(sa)
        new_b.append(sb)
        new_c.append(sc)
        new_conv.append(scv)
    return x, jnp.stack(new_a), jnp.stack(new_b), jnp.stack(new_c), jnp.stack(new_conv)


def setup_inputs(seed: int = 0) -> dict:
    key = jax.random.key(seed)
    ks = jax.random.split(key, 32)

    def nrm(k, shape, scale):
        return jax.random.normal(k, shape, F32) * scale

    dt_init = jnp.exp(jax.random.uniform(ks[11], (DEPTH, C_HEADS), F32, math.log(1e-3), math.log(1e-1)))
    dt_bias = dt_init + jnp.log(-jnp.expm1(-dt_init))
    return {
        'x_prompt': nrm(ks[0], (BATCH, SEQ, D_MODEL), 1.0),
        'x_sample': nrm(ks[1], (DEC_BATCH, DEC_SEQ, D_MODEL), 1.0),
        'state_hgrn': nrm(ks[2], (DEPTH, DEC_BATCH, A_HEADS, A_DK, A_DV), 0.5),
        'state_ret': nrm(ks[3], (DEPTH, DEC_BATCH, B_HEADS, B_DK, B_DV), 0.5),
        'state_gdn': nrm(ks[4], (DEPTH, DEC_BATCH, C_HEADS, C_DK, C_DV), 0.5),
        'state_conv': nrm(ks[5], (DEPTH, DEC_BATCH, CONV_W - 1, C_CONV), 1.0),
        'lower_bounds': nrm(ks[6], (DEPTH, A_QK), 0.1),
        'w_in': nrm(ks[7], (DEPTH, D_MODEL, PROJ_COLS), D_MODEL ** -0.5),
        'norm_a_w': 1.0 + nrm(ks[8], (DEPTH, A_DV), 0.02),
        'conv_w': nrm(ks[9], (DEPTH, CONV_W, C_CONV), CONV_W ** -0.5),
        'a_log': jnp.log(jax.random.uniform(ks[10], (DEPTH, C_HEADS), F32, 1.0, 16.0)),
        'dt_bias': dt_bias,
        'norm_c_w': 1.0 + nrm(ks[12], (DEPTH, C_DV), 0.02),
        'w_br_a': nrm(ks[13], (DEPTH, A_V, D_MODEL), A_V ** -0.5 * DEEPNORM_BETA),
        'w_br_b': nrm(ks[14], (DEPTH, B_V, D_MODEL), B_V ** -0.5 * DEEPNORM_BETA),
        'w_br_c': nrm(ks[15], (DEPTH, C_V, D_MODEL), C_V ** -0.5 * DEEPNORM_BETA),
        'w_out': nrm(ks[16], (DEPTH, D_MODEL, D_MODEL), D_MODEL ** -0.5 * DEEPNORM_BETA),
        'ln1_g': 1.0 + nrm(ks[17], (DEPTH, D_MODEL), 0.02),
        'ln1_b': nrm(ks[18], (DEPTH, D_MODEL), 0.02),
        'ln2_g': 1.0 + nrm(ks[19], (DEPTH, D_MODEL), 0.02),
        'ln2_b': nrm(ks[20], (DEPTH, D_MODEL), 0.02),
        'ffn_w1': nrm(ks[21], (N_DENSE, D_MODEL, D_FF), D_MODEL ** -0.5),
        'ffn_w3': nrm(ks[22], (N_DENSE, D_MODEL, D_FF), D_MODEL ** -0.5),
        'ffn_w2': nrm(ks[23], (N_DENSE, D_FF, D_MODEL), D_FF ** -0.5 * DEEPNORM_BETA),
        'router': nrm(ks[24], (N_MOE, D_MODEL, N_EXPERTS), D_MODEL ** -0.5),
        'moe_w1': nrm(ks[25], (N_MOE, N_EXPERTS, D_MODEL, D_FF), D_MODEL ** -0.5),
        'moe_w3': nrm(ks[26], (N_MOE, N_EXPERTS, D_MODEL, D_FF), D_MODEL ** -0.5),
        'moe_w2': nrm(ks[27], (N_MOE, N_EXPERTS, D_FF, D_MODEL), D_FF ** -0.5 * DEEPNORM_BETA),
    }


def reference(x_prompt, x_sample, state_hgrn, state_ret, state_gdn, state_conv,
              lower_bounds, w_in, norm_a_w, conv_w, a_log, dt_bias, norm_c_w,
              w_br_a, w_br_b, w_br_c, w_out, ln1_g, ln1_b, ln2_g, ln2_b,
              ffn_w1, ffn_w3, ffn_w2, router, moe_w1, moe_w3, moe_w2):
    params = (lower_bounds, w_in, norm_a_w, conv_w, a_log, dt_bias, norm_c_w, w_br_a, w_br_b, w_br_c, w_out,
              ln1_g, ln1_b, ln2_g, ln2_b, ffn_w1, ffn_w3, ffn_w2, router, moe_w1, moe_w3, moe_w2)
    dt = x_prompt.dtype
    z_hgrn = jnp.zeros((DEPTH, BATCH, A_HEADS, A_DK, A_DV), dt)
    z_ret = jnp.zeros((DEPTH, BATCH, B_HEADS, B_DK, B_DV), dt)
    z_gdn = jnp.zeros((DEPTH, BATCH, C_HEADS, C_DK, C_DV), dt)
    z_conv = jnp.zeros((DEPTH, BATCH, CONV_W - 1, C_CONV), dt)
    y_prompt, hgrn_p, ret_p, gdn_p, conv_p = _run_trunk(x_prompt, z_hgrn, z_ret, z_gdn, z_conv, 0, params)
    y_sample, hgrn_s, ret_s, gdn_s, conv_s = _run_trunk(x_sample, state_hgrn, state_ret, state_gdn, state_conv,
                                                        PAST_LEN, params)
    return (y_prompt, y_sample, hgrn_p, ret_p, gdn_p, conv_p, hgrn_s, ret_s, gdn_s, conv_s)
```

```python
import functools
import math

import jax
import jax.numpy as jnp
import numpy as np
from jax import lax
from jax.experimental import pallas as pl
from jax.experimental.pallas import tpu as pltpu

F32 = jnp.float32
BF16 = jnp.bfloat16

CHUNK = 64
TOP_K = 2
PAST_LEN = 2048
ROPE_BASE = 10000.0
LN_EPS = 1e-5
RMS_EPS = 1e-6

V7X_LANES = 128
V7X_VMEM_LIMIT_BYTES = 56 * 1024 * 1024


def _tile(n, target, align=V7X_LANES):
    if n <= target:
        return n
    t = (target // align) * align
    while t >= align:
        if n % t == 0:
            return t
        t -= align
    raise ValueError(f"no {align}-aligned tile <= {target} divides {n}")


def _params(*sem):
    return pltpu.CompilerParams(dimension_semantics=sem, vmem_limit_bytes=V7X_VMEM_LIMIT_BYTES)


def _mm_kernel(x_ref, w_ref, o_ref):
    o_ref[...] = jnp.dot(x_ref[...], w_ref[...], preferred_element_type=F32).astype(o_ref.dtype)


def _matmul(x, w, out_dtype, *, bm=1024, bn=1024, name="matmul"):
    m, k = x.shape
    n = w.shape[1]
    bm, bn = _tile(m, bm, 8), _tile(n, bn)
    return pl.pallas_call(
        _mm_kernel,
        grid=(m // bm, n // bn),
        in_specs=[pl.BlockSpec((bm, k), lambda i, j: (i, 0)), pl.BlockSpec((k, bn), lambda i, j: (0, j))],
        out_specs=pl.BlockSpec((bm, bn), lambda i, j: (i, j)),
        out_shape=jax.ShapeDtypeStruct((m, n), out_dtype),
        compiler_params=_params("parallel", "parallel"),
        name=name,
    )(x, w)


def _merge_kernel(oa_ref, ob_ref, oc_ref, wa_ref, wb_ref, wc_ref, ga_ref, gb_ref, gc_ref, o_ref):
    acc = jax.nn.sigmoid(ga_ref[...]) * jnp.dot(oa_ref[...], wa_ref[...], preferred_element_type=F32)
    acc = acc + jax.nn.sigmoid(gb_ref[...]) * jnp.dot(ob_ref[...], wb_ref[...], preferred_element_type=F32)
    acc = acc + jax.nn.sigmoid(gc_ref[...]) * jnp.dot(oc_ref[...], wc_ref[...], preferred_element_type=F32)
    o_ref[...] = acc.astype(o_ref.dtype)


def _merge(o_a, o_b, o_c, w_a, w_b, w_c, proj, gate_offs, *, bm=512, bn=1024):
    m = o_a.shape[0]
    d = w_a.shape[1]
    bm, bn = _tile(m, bm, 8), _tile(d, bn)
    assert all(off % bn == 0 for off in gate_offs)
    ga, gb, gc = (off // bn for off in gate_offs)

    def o_spec(o):
        return pl.BlockSpec((bm, o.shape[1]), lambda i, j: (i, 0))

    def w_spec(w):
        return pl.BlockSpec((w.shape[0], bn), lambda i, j: (0, j))

    def g_spec(blk):
        return pl.BlockSpec((bm, bn), lambda i, j: (i, blk + j))

    return pl.pallas_call(
        _merge_kernel,
        grid=(m // bm, d // bn),
        in_specs=[o_spec(o_a), o_spec(o_b), o_spec(o_c), w_spec(w_a), w_spec(w_b), w_spec(w_c),
                  g_spec(ga), g_spec(gb), g_spec(gc)],
        out_specs=pl.BlockSpec((bm, bn), lambda i, j: (i, j)),
        out_shape=jax.ShapeDtypeStruct((m, d), BF16),
        compiler_params=_params("parallel", "parallel"),
        name="merge",
    )(o_a, o_b, o_c, w_a, w_b, w_c, proj, proj, proj)


def _add_ln_kernel(alpha, n_y, res_ref, *refs):
    y_refs, (g_ref, b_ref, o32_ref, o16_ref) = refs[:n_y], refs[n_y:]
    x = alpha * res_ref[...]
    y = y_refs[0][...]
    for r in y_refs[1:]:
        y = y + r[...]
    x = x + y
    mu = jnp.mean(x, -1, keepdims=True)
    xc = x - mu
    var = jnp.mean(xc * xc, -1, keepdims=True)
    out = xc * lax.rsqrt(var + LN_EPS) * g_ref[...] + b_ref[...]
    o32_ref[...] = out
    o16_ref[...] = out.astype(BF16)


def _add_ln(res, ys, g, b, alpha, *, bm=256):
    m, d = res.shape
    bm = _tile(m, bm, 8)
    row = pl.BlockSpec((bm, d), lambda i: (i, 0))
    vec = pl.BlockSpec((1, d), lambda i: (0, 0))
    return pl.pallas_call(
        functools.partial(_add_ln_kernel, alpha, len(ys)),
        grid=(m // bm,),
        in_specs=[row] * (1 + len(ys)) + [vec, vec],
        out_specs=[row, row],
        out_shape=[jax.ShapeDtypeStruct((m, d), F32), jax.ShapeDtypeStruct((m, d), BF16)],
        compiler_params=_params("parallel"),
        name="add_ln",
    )(res, *ys, g.reshape(1, d), b.reshape(1, d))


def _swiglu_up_kernel(has_gate, te_ref, nu_ref, x_ref, w1_ref, w3_ref, *refs):
    o_ref = refs[-1]

    @pl.when(pl.program_id(1) < nu_ref[0])
    def _():
        x = x_ref[...]
        a = jnp.dot(x, w1_ref[...], preferred_element_type=F32)
        b = jnp.dot(x, w3_ref[...], preferred_element_type=F32)
        hid = (a * jax.nn.sigmoid(a)) * b
        if has_gate:
            hid = hid * refs[0][...]
        o_ref[...] = hid.astype(o_ref.dtype)

    @pl.when(pl.program_id(1) >= nu_ref[0])
    def _():
        o_ref[...] = jnp.zeros_like(o_ref)


def _swiglu_up(x, w1, w3, row_gate, tile_expert, n_used, *, bm, bn=1024):
    m, k = x.shape
    n = w1.shape[-1]
    bn = _tile(n, bn)
    assert m % bm == 0

    def row_blk(j, i, te, nu):
        return jnp.minimum(i, nu[0] - 1)

    in_specs = [pl.BlockSpec((bm, k), lambda j, i, te, nu: (row_blk(j, i, te, nu), 0)),
                pl.BlockSpec((None, k, bn), lambda j, i, te, nu: (te[i], 0, j)),
                pl.BlockSpec((None, k, bn), lambda j, i, te, nu: (te[i], 0, j))]
    args = [x, w1, w3]
    if row_gate is not None:
        in_specs.append(pl.BlockSpec((bm, 1), lambda j, i, te, nu: (row_blk(j, i, te, nu), 0)))
        args.append(row_gate)
    return pl.pallas_call(
        functools.partial(_swiglu_up_kernel, row_gate is not None),
        grid_spec=pltpu.PrefetchScalarGridSpec(
            num_scalar_prefetch=2,
            grid=(n // bn, m // bm),
            in_specs=in_specs,
            out_specs=pl.BlockSpec((bm, bn), lambda j, i, te, nu: (i, j)),
        ),
        out_shape=jax.ShapeDtypeStruct((m, n), BF16),
        compiler_params=_params("arbitrary", "arbitrary"),
        name="swiglu_up",
    )(tile_expert, n_used, *args)


def _mm_acc_kernel(te_ref, nu_ref, x_ref, w_ref, o_ref, acc_ref):
    k = pl.program_id(2)

    @pl.when(pl.program_id(1) < nu_ref[0])
    def _():
        @pl.when(k == 0)
        def _():
            acc_ref[...] = jnp.zeros_like(acc_ref)

        acc_ref[...] += jnp.dot(x_ref[...], w_ref[...], preferred_element_type=F32)

        @pl.when(k == pl.num_programs(2) - 1)
        def _():
            o_ref[...] = acc_ref[...].astype(o_ref.dtype)

    @pl.when(pl.program_id(1) >= nu_ref[0])
    def _():
        o_ref[...] = jnp.zeros_like(o_ref)


def _grouped_matmul(x, w, tile_expert, n_used, out_dtype, *, bm, bn=2048, bk=2048):
    m, kdim = x.shape
    n = w.shape[-1]
    bn, bk = _tile(n, bn), _tile(kdim, bk)
    assert m % bm == 0

    def row_blk(i, nu):
        return jnp.minimum(i, nu[0] - 1)

    return pl.pallas_call(
        _mm_acc_kernel,
        grid_spec=pltpu.PrefetchScalarGridSpec(
            num_scalar_prefetch=2,
            grid=(n // bn, m // bm, kdim // bk),
            in_specs=[pl.BlockSpec((bm, bk), lambda j, i, kk, te, nu: (row_blk(i, nu), kk)),
                      pl.BlockSpec((None, bk, bn), lambda j, i, kk, te, nu: (te[i], kk, j))],
            out_specs=pl.BlockSpec((bm, bn), lambda j, i, kk, te, nu: (i, j)),
            scratch_shapes=[pltpu.VMEM((bm, bn), F32)],
        ),
        out_shape=jax.ShapeDtypeStruct((m, n), out_dtype),
        compiler_params=_params("arbitrary", "arbitrary", "arbitrary"),
        name="grouped_matmul",
    )(tile_expert, n_used, x, w)


def _router_kernel(n_experts, x_ref, w_ref, idx_ref, gate_ref):
    x = x_ref[...]
    w = w_ref[...]
    x_hi = x.astype(BF16)
    x_lo = (x - x_hi.astype(F32)).astype(BF16)
    w_hi = w.astype(BF16)
    w_lo = (w - w_hi.astype(F32)).astype(BF16)
    logits = (jnp.dot(x_hi, w_lo, preferred_element_type=F32) + jnp.dot(x_lo, w_hi, preferred_element_type=F32)
              + jnp.dot(x_hi, w_hi, preferred_element_type=F32))
    lane = lax.broadcasted_iota(jnp.int32, logits.shape, 1)
    neg = jnp.float32(-jnp.inf)
    l1 = jnp.where(lane < n_experts, logits, neg)
    m1 = jnp.max(l1, -1, keepdims=True)
    i1 = jnp.min(jnp.where(l1 == m1, lane, V7X_LANES), -1, keepdims=True)
    l2 = jnp.where(lane == i1, neg, l1)
    m2 = jnp.max(l2, -1, keepdims=True)
    i2 = jnp.min(jnp.where(l2 == m2, lane, V7X_LANES), -1, keepdims=True)
    e2 = jnp.exp(m2 - m1)
    den = 1.0 + e2
    idx_ref[...] = jnp.where(lane == 0, i1, jnp.where(lane == 1, i2, 0))
    gate_ref[...] = jnp.where(lane == 0, 1.0 / den, jnp.where(lane == 1, e2 / den, 0.0))


def _router(x, w_router, *, bm=512):
    m, d = x.shape
    n_experts = w_router.shape[1]
    bm = _tile(m, bm, 8)
    w_pad = jnp.pad(w_router, ((0, 0), (0, V7X_LANES - n_experts)))
    blk = pl.BlockSpec((bm, V7X_LANES), lambda i: (i, 0))
    idx, gate = pl.pallas_call(
        functools.partial(_router_kernel, n_experts),
        grid=(m // bm,),
        in_specs=[pl.BlockSpec((bm, d), lambda i: (i, 0)), pl.BlockSpec((d, V7X_LANES), lambda i: (0, 0))],
        out_specs=[blk, blk],
        out_shape=[jax.ShapeDtypeStruct((m, V7X_LANES), jnp.int32), jax.ShapeDtypeStruct((m, V7X_LANES), F32)],
        compiler_params=_params("parallel"),
        name="router",
    )(x, w_pad)
    return idx[:, :TOP_K], gate[:, :TOP_K]


def _gather_kernel(bm, idx_ref, src_ref, o_ref, buf_ref, sem):
    base = pl.program_id(0) * bm

    def row_copy(r, src_row):
        return pltpu.make_async_copy(src_ref.at[pl.ds(src_row, 1)], buf_ref.at[pl.ds(r, 1)], sem)

    def start(r, carry):
        row_copy(r, idx_ref[base + r]).start()
        return carry

    def wait(r, carry):
        row_copy(r, 0).wait()
        return carry

    lax.fori_loop(0, bm, start, 0)
    lax.fori_loop(0, bm, wait, 0)
    o_ref[...] = buf_ref[...].astype(o_ref.dtype)


def _gather_rows(src, idx, out_dtype, *, bm=256):
    n = idx.shape[0]
    d = src.shape[1]
    bm = _tile(n, bm, 8)
    return pl.pallas_call(
        functools.partial(_gather_kernel, bm),
        grid_spec=pltpu.PrefetchScalarGridSpec(
            num_scalar_prefetch=1,
            grid=(n // bm,),
            in_specs=[pl.BlockSpec(memory_space=pl.ANY)],
            out_specs=pl.BlockSpec((bm, d), lambda i, idx: (i, 0)),
            scratch_shapes=[pltpu.VMEM((bm, d), src.dtype), pltpu.SemaphoreType.DMA(())],
        ),
        out_shape=jax.ShapeDtypeStruct((n, d), out_dtype),
        compiler_params=_params("arbitrary"),
        name="gather_rows",
    )(idx, src)


def _dense_ffn(x16, w1, w3, w2, *, bm=512):
    m = x16.shape[0]
    bm = _tile(m, bm, 8)
    tile_expert = jnp.zeros((m // bm,), jnp.int32)
    n_used = jnp.full((1,), m // bm, jnp.int32)
    hid = _swiglu_up(x16, w1, w3, None, tile_expert, n_used, bm=bm)
    return _grouped_matmul(hid, w2, tile_expert, n_used, F32, bm=bm)


def _moe_ffn(x32, x16, w_router, w1, w3, w2, *, bm=512):
    t = x32.shape[0]
    n_experts = w_router.shape[1]
    bm = _tile(t, bm, 8)
    top_idx, top_w = _router(x32, w_router)
    flat_e = top_idx.reshape(-1)
    onehot = (flat_e[:, None] == jnp.arange(n_experts, dtype=jnp.int32)[None, :]).astype(jnp.int32)
    rank = jnp.sum((jnp.cumsum(onehot, axis=0) - onehot) * onehot, axis=1)
    counts = jnp.sum(onehot, axis=0)
    padded = ((counts + bm - 1) // bm) * bm
    ends = jnp.cumsum(padded)
    pos = (ends - padded)[flat_e] + rank
    n_rows = TOP_K * t + n_experts * bm
    n_tiles = n_rows // bm
    n_used = (ends[-1] // bm).astype(jnp.int32).reshape(1)
    tile_start = jnp.minimum(jnp.arange(n_tiles, dtype=jnp.int32), n_used[0] - 1) * bm
    tile_expert = jnp.minimum(jnp.searchsorted(ends, tile_start, side="right"), n_experts - 1).astype(jnp.int32)
    token_of_row = jnp.arange(TOP_K * t, dtype=jnp.int32) // TOP_K
    slot_token = jnp.zeros((n_rows,), jnp.int32).at[pos].set(token_of_row)
    slot_gate = jnp.zeros((n_rows,), F32).at[pos].set(top_w.reshape(-1))

    x_sorted = _gather_rows(x32, slot_token, BF16)
    hid = _swiglu_up(x_sorted, w1, w3, slot_gate.reshape(n_rows, 1), tile_expert, n_used, bm=bm)
    out_sorted = _grouped_matmul(hid, w2, tile_expert, n_used, F32, bm=bm)
    pos2 = pos.reshape(t, TOP_K).astype(jnp.int32)
    return [_gather_rows(out_sorted, pos2[:, k], F32) for k in range(TOP_K)]


def _rms_norm(x, g):
    return x * lax.rsqrt(jnp.mean(jnp.square(x), -1, keepdims=True) + RMS_EPS) * g


def _group_norm(x):
    mu = jnp.mean(x, -1, keepdims=True)
    var = jnp.mean(jnp.square(x - mu), -1, keepdims=True)
    return (x - mu) * lax.rsqrt(var + LN_EPS)


def _l2norm(x):
    return x * lax.rsqrt(jnp.sum(x * x, -1, keepdims=True) + RMS_EPS)


def _rope(x, pos):
    half = x.shape[-1] // 2
    inv_freq = ROPE_BASE ** (-jnp.arange(half, dtype=F32) / half)
    ang = pos.astype(F32)[:, None] * inv_freq[None, :]
    cos = jnp.cos(ang)[None, :, None, :]
    sin = jnp.sin(ang)[None, :, None, :]
    x1, x2 = x[..., :half], x[..., half:]
    return jnp.concatenate([x1 * cos - x2 * sin, x1 * sin + x2 * cos], -1)


def _to_chunks(x, c):
    b, t, h, d = x.shape
    return x.reshape(b, t // c, c, h, d).transpose(1, 0, 3, 2, 4)


def _from_chunks(x):
    nc, b, h, c, d = x.shape
    return x.transpose(1, 0, 3, 2, 4).reshape(b, nc * c, h, d)


def _chunk_gla(q, k, v, log_f, s0, c):
    idx = jnp.arange(c)
    causal = (idx[:, None] >= idx[None, :])[:, :, None]

    def step(s, inp):
        qc, kc, vc, gc = inp
        bcum = jnp.cumsum(gc, axis=2)
        diff = bcum[:, :, :, None, :] - bcum[:, :, None, :, :]
        dec = jnp.exp(jnp.where(causal, diff, -jnp.inf))
        scores = jnp.einsum('bhtsd,bhsd->bhts', qc[:, :, :, None, :] * dec, kc)
        o = (jnp.einsum('bhts,bhsv->bhtv', scores, vc) + jnp.einsum('bhtd,bhdv->bhtv', qc * jnp.exp(bcum), s))
        blast = bcum[:, :, -1:, :]
        s_new = (jnp.exp(blast[:, :, 0, :])[..., None] * s
                 + jnp.einsum('bhsd,bhsv->bhdv', kc * jnp.exp(blast - bcum), vc))
        return s_new, o

    s_fin, o = lax.scan(step, s0, (_to_chunks(q, c), _to_chunks(k, c), _to_chunks(v, c), _to_chunks(log_f, c)))
    return _from_chunks(o), s_fin


def _chunk_retention(q, k, v, log_gamma, s0, c):
    idx = jnp.arange(c, dtype=F32)
    rel = idx[:, None] - idx[None, :]
    lg = log_gamma[:, None, None]
    intra = jnp.where(rel >= 0, jnp.exp(jnp.maximum(rel, 0.0) * lg), 0.0)
    inner = jnp.exp((idx + 1.0)[None, :] * log_gamma[:, None])
    tail = jnp.exp((c - 1.0 - idx)[None, :] * log_gamma[:, None])
    whole = jnp.exp(c * log_gamma)

    def step(s, inp):
        qc, kc, vc = inp
        scores = jnp.einsum('bhtd,bhsd->bhts', qc, kc) * intra
        o = (jnp.einsum('bhts,bhsv->bhtv', scores, vc) + jnp.einsum('bhtd,bhdv->bhtv', qc, s) * inner[:, :, None])
        s_new = whole[:, None, None] * s + jnp.einsum('bhsd,bhsv->bhdv', kc * tail[:, :, None], vc)
        return s_new, o

    s_fin, o = lax.scan(step, s0, (_to_chunks(q, c), _to_chunks(k, c), _to_chunks(v, c)))
    return _from_chunks(o), s_fin


def _chunk_gated_delta(q, k, v, g, beta, s0, c):
    idx = jnp.arange(c)
    incl = idx[:, None] >= idx[None, :]
    strict = idx[:, None] > idx[None, :]
    eye = jnp.eye(c, dtype=F32)
    dv = v.shape[-1]

    def step(s, inp):
        qc, kc, vc, gc, bc = inp
        bcum = jnp.cumsum(gc[..., 0], axis=-1)
        diff = bcum[..., :, None] - bcum[..., None, :]
        lmask = jnp.where(incl, jnp.exp(jnp.where(incl, diff, 0.0)), 0.0)
        kb = kc * bc
        a_low = jnp.where(strict, jnp.einsum('bhtd,bhsd->bhts', kb, kc) * lmask, 0.0)
        rhs = jnp.concatenate([vc * bc, kb * jnp.exp(bcum)[..., None]], -1)
        sol = lax.linalg.triangular_solve(eye + a_low, rhs, left_side=True, lower=True, unit_diagonal=True)
        u, w = sol[..., :dv], sol[..., dv:]
        v_new = u - jnp.einsum('bhtd,bhdv->bhtv', w, s)
        attn = jnp.einsum('bhtd,bhsd->bhts', qc, kc) * lmask
        o = (jnp.einsum('bhts,bhsv->bhtv', attn, v_new)
             + jnp.einsum('bhtd,bhdv->bhtv', qc * jnp.exp(bcum)[..., None], s))
        blast = bcum[..., -1:]
        s_new = (jnp.exp(blast)[..., None] * s
                 + jnp.einsum('bhsd,bhsv->bhdv', kc * jnp.exp(blast - bcum)[..., None], v_new))
        return s_new, o

    xs = (_to_chunks(q, c), _to_chunks(k, c), _to_chunks(v, c), _to_chunks(g, c), _to_chunks(beta, c))
    s_fin, o = lax.scan(step, s0, xs)
    return _from_chunks(o), s_fin


def _mixer_branches(proj, small, seg, bsz, t, s_a, s_b, s_c, s_conv, pos0, lb, norm_a_w, conv_w, a_log, dt_bias,
                    norm_c_w):
    c = math.gcd(t, CHUNK)
    a_heads, b_heads, c_heads = s_a.shape[1], s_b.shape[1], s_c.shape[1]

    def col(name):
        off, size = seg[name]
        return proj[:, off:off + size].reshape(bsz, t, size)

    def heads(z, n):
        return z.reshape(bsz, t, n, -1)

    lbh = jnp.maximum(lb, 0.0).reshape(a_heads, -1)
    zf = heads(col("a_f"), a_heads)
    log_f = jnp.logaddexp(jnp.log(lbh), jnp.log1p(-lbh) + jax.nn.log_sigmoid(zf))
    k_a = (1.0 - lbh) * jax.nn.sigmoid(-zf)
    o_a, s_a_new = _chunk_gla(jax.nn.silu(heads(col("a_q"), a_heads)), k_a, heads(col("a_i"), a_heads), log_f, s_a, c)
    o_a = _rms_norm(o_a, norm_a_w) * jax.nn.silu(heads(col("a_g"), a_heads))

    pos = pos0 + jnp.arange(t)
    b_dk = s_b.shape[2]
    q_b = _rope(heads(col("b_q"), b_heads), pos)
    k_b = _rope(heads(col("b_k"), b_heads), pos) * (b_dk ** -0.5)
    log_gamma = jnp.log1p(-jnp.exp2(-5.0 - jnp.arange(b_heads, dtype=F32)))
    o_b, s_b_new = _chunk_retention(q_b, k_b, heads(col("b_v"), b_heads), log_gamma, s_b, c)
    o_b = _group_norm(o_b) * jax.nn.silu(heads(col("b_g"), b_heads))

    c_qkv = col("c_qkv")
    conv_width = conv_w.shape[0]
    conv_in = jnp.concatenate([s_conv, c_qkv], axis=1)
    s_conv_new = conv_in[:, -(conv_width - 1):]
    qkv = sum(conv_in[:, w:w + t] * conv_w[w][None, None, :] for w in range(conv_width))
    qkv = jax.nn.silu(qkv)
    c_qk = s_c.shape[1] * s_c.shape[2]
    c_dk = s_c.shape[2]
    q_c, k_c, v_c = jnp.split(qkv, [c_qk, 2 * c_qk], axis=-1)
    q_c = _l2norm(heads(q_c, c_heads)) * (c_dk ** -0.5)
    k_c = _l2norm(heads(k_c, c_heads))
    c_a = small[:, :c_heads].reshape(bsz, t, c_heads)
    c_b = small[:, c_heads:2 * c_heads].reshape(bsz, t, c_heads)
    beta = jax.nn.sigmoid(c_b)[..., None]
    g = (-jnp.exp(a_log) * jax.nn.softplus(c_a + dt_bias))[..., None]
    o_c, s_c_new = _chunk_gated_delta(q_c, k_c, heads(v_c, c_heads), g, beta, s_c, c)
    o_c = _rms_norm(o_c, norm_c_w) * jax.nn.silu(heads(col("c_g"), c_heads))

    def flat(o):
        return o.reshape(bsz * t, -1).astype(BF16)

    return flat(o_a), flat(o_b), flat(o_c), s_a_new, s_b_new, s_c_new, s_conv_new


def kernel(x_prompt, x_sample, state_hgrn, state_ret, state_gdn, state_conv, lower_bounds, w_in, norm_a_w, conv_w,
           a_log, dt_bias, norm_c_w, w_br_a, w_br_b, w_br_c, w_out, ln1_g, ln1_b, ln2_g, ln2_b, ffn_w1, ffn_w3,
           ffn_w2, router, moe_w1, moe_w3, moe_w2):
    bp, tp, d = x_prompt.shape
    bs, ts, _ = x_sample.shape
    depth = state_hgrn.shape[0]
    _, _, a_heads, a_dk, a_dv = state_hgrn.shape
    _, _, b_heads, b_dk, b_dv = state_ret.shape
    _, _, c_heads, c_dk, c_dv = state_gdn.shape
    conv_width = conv_w.shape[1]
    alpha = (2 * depth) ** 0.25
    n_p = bp * tp

    a_qk, a_v = a_heads * a_dk, a_heads * a_dv
    b_qk, b_v = b_heads * b_dk, b_heads * b_dv
    c_qk, c_v = c_heads * c_dk, c_heads * c_dv
    sizes = [("a_q", a_qk), ("a_f", a_qk), ("a_i", a_v), ("a_g", a_v), ("b_q", b_qk), ("b_k", b_qk), ("b_v", b_v),
             ("b_g", b_v), ("c_qkv", 2 * c_qk + c_v), ("c_a", c_heads), ("c_b", c_heads), ("c_g", c_v),
             ("gate_a", d), ("gate_b", d), ("gate_c", d)]
    src_off, off = {}, 0
    for name, size in sizes:
        src_off[name] = off
        off += size
    seg, off = {}, 0
    for name, size in sizes:
        if name in ("c_a", "c_b"):
            continue
        seg[name] = (off, size)
        off += size
    n_small = 2 * c_heads
    lo, hi = src_off["c_a"], src_off["c_g"]
    w_main = jnp.concatenate([w_in[:, :, :lo], w_in[:, :, hi:]], axis=-1).astype(BF16)
    w_small = jnp.pad(w_in[:, :, lo:hi], ((0, 0), (0, 0), (0, V7X_LANES - n_small))).astype(BF16)

    p = jax.nn.softmax(lower_bounds.astype(F32), axis=0)
    lb_all = jnp.cumsum(p, axis=0) - p[0:1]

    x = jnp.concatenate([x_prompt.reshape(n_p, d), x_sample.reshape(bs * ts, d)], axis=0)
    x16 = x.astype(BF16)
    zeros = lambda b, s: jnp.zeros((b,) + s.shape[2:], F32)
    new_states = [[] for _ in range(8)]
    for l in range(depth):
        proj = _matmul(x16, w_main[l], F32, name="in_proj")
        small = _matmul(x16, w_small[l], F32, name="in_proj_small")
        branch_args = (lb_all[l], norm_a_w[l], conv_w[l], a_log[l], dt_bias[l], norm_c_w[l])
        out_p = _mixer_branches(proj[:n_p], small[:n_p], seg, bp, tp, zeros(bp, state_hgrn), zeros(bp, state_ret),
                                zeros(bp, state_gdn), zeros(bp, state_conv), 0, *branch_args)
        out_s = _mixer_branches(proj[n_p:], small[n_p:], seg, bs, ts, state_hgrn[l], state_ret[l], state_gdn[l],
                                state_conv[l], PAST_LEN, *branch_args)
        o_a, o_b, o_c = (jnp.concatenate([out_p[i], out_s[i]], axis=0) for i in range(3))
        for i in range(4):
            new_states[i].append(out_p[3 + i])
            new_states[4 + i].append(out_s[3 + i])
        merged = _merge(o_a, o_b, o_c, w_br_a[l].astype(BF16), w_br_b[l].astype(BF16), w_br_c[l].astype(BF16),
                        proj, [seg[n][0] for n in ("gate_a", "gate_b", "gate_c")])
        y = _matmul(merged, w_out[l].astype(BF16), F32, name="out_proj")
        x, x16 = _add_ln(x, [y], ln1_g[l], ln1_b[l], alpha)
        if l % 2 == 0:
            j = l // 2
            f = [_dense_ffn(x16, ffn_w1[j:j + 1].astype(BF16), ffn_w3[j:j + 1].astype(BF16),
                            ffn_w2[j:j + 1].astype(BF16))]
        else:
            j = l // 2
            f = _moe_ffn(x, x16, router[j], moe_w1[j].astype(BF16), moe_w3[j].astype(BF16), moe_w2[j].astype(BF16))
        x, x16 = _add_ln(x, f, ln2_g[l], ln2_b[l], alpha)

    y_prompt = x[:n_p].reshape(bp, tp, d)
    y_sample = x[n_p:].reshape(bs, ts, d)
    return (y_prompt, y_sample) + tuple(jnp.stack(s) for s in new_states)
```

```python
import functools
import math

import jax
import jax.numpy as jnp
import numpy as np
from jax import lax
from jax.experimental import pallas as pl
from jax.experimental.pallas import tpu as pltpu

F32 = jnp.float32
BF16 = jnp.bfloat16

CHUNK = 64
TOP_K = 2
PAST_LEN = 2048
ROPE_BASE = 10000.0
LN_EPS = 1e-5
RMS_EPS = 1e-6

V7X_LANES = 128
V7X_VMEM_LIMIT_BYTES = 56 * 1024 * 1024


def _tile(n, target, align=V7X_LANES):
    if n <= target:
        return n
    t = (target // align) * align
    while t >= align:
        if n % t == 0:
            return t
        t -= align
    raise ValueError(f"no {align}-aligned tile <= {target} divides {n}")


def _params(*sem):
    return pltpu.CompilerParams(dimension_semantics=sem, vmem_limit_bytes=V7X_VMEM_LIMIT_BYTES)


def _mm_kernel(x_ref, w_ref, o_ref):
    o_ref[...] = jnp.dot(x_ref[...], w_ref[...], preferred_element_type=F32).astype(o_ref.dtype)


def _matmul(x, w, out_dtype, *, bm=1024, bn=1024, name="matmul"):
    m, k = x.shape
    n = w.shape[1]
    bm, bn = _tile(m, bm, 8), _tile(n, bn)
    return pl.pallas_call(
        _mm_kernel,
        grid=(m // bm, n // bn),
        in_specs=[pl.BlockSpec((bm, k), lambda i, j: (i, 0)), pl.BlockSpec((k, bn), lambda i, j: (0, j))],
        out_specs=pl.BlockSpec((bm, bn), lambda i, j: (i, j)),
        out_shape=jax.ShapeDtypeStruct((m, n), out_dtype),
        compiler_params=_params("parallel", "parallel"),
        name=name,
    )(x, w)


def _merge_kernel(oa_ref, ob_ref, oc_ref, wa_ref, wb_ref, wc_ref, ga_ref, gb_ref, gc_ref, o_ref):
    acc = jax.nn.sigmoid(ga_ref[...]) * jnp.dot(oa_ref[...], wa_ref[...], preferred_element_type=F32)
    acc = acc + jax.nn.sigmoid(gb_ref[...]) * jnp.dot(ob_ref[...], wb_ref[...], preferred_element_type=F32)
    acc = acc + jax.nn.sigmoid(gc_ref[...]) * jnp.dot(oc_ref[...], wc_ref[...], preferred_element_type=F32)
    o_ref[...] = acc.astype(o_ref.dtype)


def _merge(o_a, o_b, o_c, w_a, w_b, w_c, proj, gate_offs, *, bm=512, bn=1024):
    m = o_a.shape[0]
    d = w_a.shape[1]
    bm, bn = _tile(m, bm, 8), _tile(d, bn)
    assert all(off % bn == 0 for off in gate_offs)
    ga, gb, gc = (off // bn for off in gate_offs)

    def o_spec(o):
        return pl.BlockSpec((bm, o.shape[1]), lambda i, j: (i, 0))

    def w_spec(w):
        return pl.BlockSpec((w.shape[0], bn), lambda i, j: (0, j))

    def g_spec(blk):
        return pl.BlockSpec((bm, bn), lambda i, j: (i, blk + j))

    return pl.pallas_call(
        _merge_kernel,
        grid=(m // bm, d // bn),
        in_specs=[o_spec(o_a), o_spec(o_b), o_spec(o_c), w_spec(w_a), w_spec(w_b), w_spec(w_c),
                  g_spec(ga), g_spec(gb), g_spec(gc)],
        out_specs=pl.BlockSpec((bm, bn), lambda i, j: (i, j)),
        out_shape=jax.ShapeDtypeStruct((m, d), BF16),
        compiler_params=_params("parallel", "parallel"),
        name="merge",
    )(o_a, o_b, o_c, w_a, w_b, w_c, proj, proj, proj)


def _add_ln_kernel(alpha, n_y, res_ref, *refs):
    y_refs, (g_ref, b_ref, o32_ref, o16_ref) = refs[:n_y], refs[n_y:]
    x = alpha * res_ref[...]
    y = y_refs[0][...]
    for r in y_refs[1:]:
        y = y + r[...]
    x = x + y
    mu = jnp.mean(x, -1, keepdims=True)
    xc = x - mu
    var = jnp.mean(xc * xc, -1, keepdims=True)
    out = xc * lax.rsqrt(var + LN_EPS) * g_ref[...] + b_ref[...]
    o32_ref[...] = out
    o16_ref[...] = out.astype(BF16)


def _add_ln(res, ys, g, b, alpha, *, bm=256):
    m, d = res.shape
    bm = _tile(m, bm, 8)
    row = pl.BlockSpec((bm, d), lambda i: (i, 0))
    vec = pl.BlockSpec((1, d), lambda i: (0, 0))
    return pl.pallas_call(
        functools.partial(_add_ln_kernel, alpha, len(ys)),
        grid=(m // bm,),
        in_specs=[row] * (1 + len(ys)) + [vec, vec],
        out_specs=[row, row],
        out_shape=[jax.ShapeDtypeStruct((m, d), F32), jax.ShapeDtypeStruct((m, d), BF16)],
        compiler_params=_params("parallel"),
        name="add_ln",
    )(res, *ys, g.reshape(1, d), b.reshape(1, d))


def _swiglu_up_kernel(has_gate, te_ref, nu_ref, x_ref, w1_ref, w3_ref, *refs):
    o_ref = refs[-1]

    @pl.when(pl.program_id(1) < nu_ref[0])
    def _():
        x = x_ref[...]
        a = jnp.dot(x, w1_ref[...], preferred_element_type=F32)
        b = jnp.dot(x, w3_ref[...], preferred_element_type=F32)
        hid = (a * jax.nn.sigmoid(a)) * b
        if has_gate:
            hid = hid * refs[0][...]
        o_ref[...] = hid.astype(o_ref.dtype)

    @pl.when(pl.program_id(1) >= nu_ref[0])
    def _():
        o_ref[...] = jnp.zeros_like(o_ref)


def _swiglu_up(x, w1, w3, row_gate, tile_expert, n_used, *, bm, bn=1024):
    m, k = x.shape
    n = w1.shape[-1]
    bn = _tile(n, bn)
    assert m % bm == 0

    def row_blk(j, i, te, nu):
        return jnp.minimum(i, nu[0] - 1)

    in_specs = [pl.BlockSpec((bm, k), lambda j, i, te, nu: (row_blk(j, i, te, nu), 0)),
                pl.BlockSpec((None, k, bn), lambda j, i, te, nu: (te[i], 0, j)),
                pl.BlockSpec((None, k, bn), lambda j, i, te, nu: (te[i], 0, j))]
    args = [x, w1, w3]
    if row_gate is not None:
        in_specs.append(pl.BlockSpec((bm, 1), lambda j, i, te, nu: (row_blk(j, i, te, nu), 0)))
        args.append(row_gate)
    return pl.pallas_call(
        functools.partial(_swiglu_up_kernel, row_gate is not None),
        grid_spec=pltpu.PrefetchScalarGridSpec(
            num_scalar_prefetch=2,
            grid=(n // bn, m // bm),
            in_specs=in_specs,
            out_specs=pl.BlockSpec((bm, bn), lambda j, i, te, nu: (i, j)),
        ),
        out_shape=jax.ShapeDtypeStruct((m, n), BF16),
        compiler_params=_params("arbitrary", "arbitrary"),
        name="swiglu_up",
    )(tile_expert, n_used, *args)


def _mm_acc_kernel(te_ref, nu_ref, x_ref, w_ref, o_ref, acc_ref):
    k = pl.program_id(2)

    @pl.when(pl.program_id(1) < nu_ref[0])
    def _():
        @pl.when(k == 0)
        def _():
            acc_ref[...] = jnp.zeros_like(acc_ref)

        acc_ref[...] += jnp.dot(x_ref[...], w_ref[...], preferred_element_type=F32)

        @pl.when(k == pl.num_programs(2) - 1)
        def _():
            o_ref[...] = acc_ref[...].astype(o_ref.dtype)

    @pl.when(pl.program_id(1) >= nu_ref[0])
    def _():
        o_ref[...] = jnp.zeros_like(o_ref)


def _grouped_matmul(x, w, tile_expert, n_used, out_dtype, *, bm, bn=2048, bk=2048):
    m, kdim = x.shape
    n = w.shape[-1]
    bn, bk = _tile(n, bn), _tile(kdim, bk)
    assert m % bm == 0

    def row_blk(i, nu):
        return jnp.minimum(i, nu[0] - 1)

    return pl.pallas_call(
        _mm_acc_kernel,
        grid_spec=pltpu.PrefetchScalarGridSpec(
            num_scalar_prefetch=2,
            grid=(n // bn, m // bm, kdim // bk),
            in_specs=[pl.BlockSpec((bm, bk), lambda j, i, kk, te, nu: (row_blk(i, nu), kk)),
                      pl.BlockSpec((None, bk, bn), lambda j, i, kk, te, nu: (te[i], kk, j))],
            out_specs=pl.BlockSpec((bm, bn), lambda j, i, kk, te, nu: (i, j)),
            scratch_shapes=[pltpu.VMEM((bm, bn), F32)],
        ),
        out_shape=jax.ShapeDtypeStruct((m, n), out_dtype),
        compiler_params=_params("arbitrary", "arbitrary", "arbitrary"),
        name="grouped_matmul",
    )(tile_expert, n_used, x, w)


def _router_kernel(n_experts, x_ref, w_ref, idx_ref, gate_ref):
    x = x_ref[...]
    w = w_ref[...]
    x_hi = x.astype(BF16)
    x_lo = (x - x_hi.astype(F32)).astype(BF16)
    w_hi = w.astype(BF16)
    w_lo = (w - w_hi.astype(F32)).astype(BF16)
    logits = (jnp.dot(x_hi, w_lo, preferred_element_type=F32) + jnp.dot(x_lo, w_hi, preferred_element_type=F32)
              + jnp.dot(x_hi, w_hi, preferred_element_type=F32))
    lane = lax.broadcasted_iota(jnp.int32, logits.shape, 1)
    neg = jnp.float32(-jnp.inf)
    l1 = jnp.where(lane < n_experts, logits, neg)
    m1 = jnp.max(l1, -1, keepdims=True)
    i1 = jnp.min(jnp.where(l1 == m1, lane, V7X_LANES), -1, keepdims=True)
    l2 = jnp.where(lane == i1, neg, l1)
    m2 = jnp.max(l2, -1, keepdims=True)
    i2 = jnp.min(jnp.where(l2 == m2, lane, V7X_LANES), -1, keepdims=True)
    e2 = jnp.exp(m2 - m1)
    den = 1.0 + e2
    idx_ref[...] = jnp.where(lane == 0, i1, jnp.where(lane == 1, i2, 0))
    gate_ref[...] = jnp.where(lane == 0, 1.0 / den, jnp.where(lane == 1, e2 / den, 0.0))


def _router(x, w_router, *, bm=512):
    m, d = x.shape
    n_experts = w_router.shape[1]
    bm = _tile(m, bm, 8)
    w_pad = jnp.pad(w_router, ((0, 0), (0, V7X_LANES - n_experts)))
    blk = pl.BlockSpec((bm, V7X_LANES), lambda i: (i, 0))
    idx, gate = pl.pallas_call(
        functools.partial(_router_kernel, n_experts),
        grid=(m // bm,),
        in_specs=[pl.BlockSpec((bm, d), lambda i: (i, 0)), pl.BlockSpec((d, V7X_LANES), lambda i: (0, 0))],
        out_specs=[blk, blk],
        out_shape=[jax.ShapeDtypeStruct((m, V7X_LANES), jnp.int32), jax.ShapeDtypeStruct((m, V7X_LANES), F32)],
        compiler_params=_params("parallel"),
        name="router",
    )(x, w_pad)
    return idx[:, :TOP_K], gate[:, :TOP_K]


def _gather_kernel(bm, idx_ref, src_ref, o_ref, buf_ref, sem):
    base = pl.program_id(0) * bm

    def row_copy(r, src_row):
        return pltpu.make_async_copy(src_ref.at[pl.ds(src_row, 1)], buf_ref.at[pl.ds(r, 1)], sem)

    def start(r, carry):
        row_copy(r, idx_ref[base + r]).start()
        return carry

    def wait(r, carry):
        row_copy(r, 0).wait()
        return carry

    lax.fori_loop(0, bm, start, 0)
    lax.fori_loop(0, bm, wait, 0)
    o_ref[...] = buf_ref[...].astype(o_ref.dtype)


def _gather_rows(src, idx, out_dtype, *, bm=256):
    n = idx.shape[0]
    d = src.shape[1]
    bm = _tile(n, bm, 8)
    return pl.pallas_call(
        functools.partial(_gather_kernel, bm),
        grid_spec=pltpu.PrefetchScalarGridSpec(
            num_scalar_prefetch=1,
            grid=(n // bm,),
            in_specs=[pl.BlockSpec(memory_space=pl.ANY)],
            out_specs=pl.BlockSpec((bm, d), lambda i, idx: (i, 0)),
            scratch_shapes=[pltpu.VMEM((bm, d), src.dtype), pltpu.SemaphoreType.DMA(())],
        ),
        out_shape=jax.ShapeDtypeStruct((n, d), out_dtype),
        compiler_params=_params("arbitrary"),
        name="gather_rows",
    )(idx, src)


def _dense_ffn(x16, w1, w3, w2, *, bm=512):
    m = x16.shape[0]
    bm = _tile(m, bm, 8)
    tile_expert = jnp.zeros((m // bm,), jnp.int32)
    n_used = jnp.full((1,), m // bm, jnp.int32)
    hid = _swiglu_up(x16, w1, w3, None, tile_expert, n_used, bm=bm)
    return _grouped_matmul(hid, w2, tile_expert, n_used, F32, bm=bm)


def _moe_ffn(x32, x16, w_router, w1, w3, w2, *, bm=512):
    t = x32.shape[0]
    n_experts = w_router.shape[1]
    bm = _tile(t, bm, 8)
    top_idx, top_w = _router(x32, w_router)
    flat_e = top_idx.reshape(-1)
    onehot = (flat_e[:, None] == jnp.arange(n_experts, dtype=jnp.int32)[None, :]).astype(jnp.int32)
    rank = jnp.sum((jnp.cumsum(onehot, axis=0) - onehot) * onehot, axis=1)
    counts = jnp.sum(onehot, axis=0)
    padded = ((counts + bm - 1) // bm) * bm
    ends = jnp.cumsum(padded)
    pos = (ends - padded)[flat_e] + rank
    n_rows = TOP_K * t + n_experts * bm
    n_tiles = n_rows // bm
    n_used = (ends[-1] // bm).astype(jnp.int32).reshape(1)
    tile_start = jnp.minimum(jnp.arange(n_tiles, dtype=jnp.int32), n_used[0] - 1) * bm
    tile_expert = jnp.minimum(jnp.searchsorted(ends, tile_start, side="right"), n_experts - 1).astype(jnp.int32)
    token_of_row = jnp.arange(TOP_K * t, dtype=jnp.int32) // TOP_K
    slot_token = jnp.zeros((n_rows,), jnp.int32).at[pos].set(token_of_row)
    slot_gate = jnp.zeros((n_rows,), F32).at[pos].set(top_w.reshape(-1))

    x_sorted = _gather_rows(x32, slot_token, BF16)
    hid = _swiglu_up(x_sorted, w1, w3, slot_gate.reshape(n_rows, 1), tile_expert, n_used, bm=bm)
    out_sorted = _grouped_matmul(hid, w2, tile_expert, n_used, F32, bm=bm)
    pos2 = pos.reshape(t, TOP_K).astype(jnp.int32)
    return [_gather_rows(out_sorted, pos2[:, k], F32) for k in range(TOP_K)]


MIXER_HEADS_PER_STEP = 4


def _silu(x):
    return x * jax.nn.sigmoid(x)


def _dot(a, b):
    return jnp.dot(a.astype(BF16), b.astype(BF16), preferred_element_type=F32)


def _dot_nt(a, b):
    return lax.dot_general(a.astype(BF16), b.astype(BF16), (((1,), (1,)), ((), ())), preferred_element_type=F32)


def _dot_tn(a, b):
    return lax.dot_general(a.astype(BF16), b.astype(BF16), (((0,), (0,)), ((), ())), preferred_element_type=F32)


def _split2(x):
    hi = x.astype(BF16)
    lo = (x - hi.astype(F32)).astype(BF16)
    return hi, lo


def _softplus(x):
    return jnp.maximum(x, 0.0) + jnp.log1p(jnp.exp(-jnp.abs(x)))


class _Group:
    def __init__(self, row0, bsz, t):
        self.row0, self.bsz, self.t = row0, bsz, t
        self.c = math.gcd(t, CHUNK)
        self.nc = t // self.c
        assert row0 % self.c == 0
        self.blk0 = row0 // self.c

    def row_blk(self, b, ci):
        return self.blk0 + b * self.nc + ci


def _gla_consts(c):
    t = np.arange(c)[:, None]
    u = np.arange(c)[None, :]
    mats, masks, levels = [u <= t], [], []
    m = c // 2
    while m >= 1:
        same = (t // (2 * m)) == (u // (2 * m))
        t2, u2 = (t % (2 * m)) >= m, (u % (2 * m)) >= m
        mats.append(same & ((t2 & u2 & (u <= t)) | (~t2 & ~u2 & (u > t))))
        masks.append(same & t2 & ~u2)
        levels.append(m)
        m //= 2
    mats.append(u > t)
    masks.append(t == u)
    return (jnp.asarray(np.concatenate(mats, 0), BF16), jnp.asarray(np.stack(masks), F32), tuple(levels))


def _gla_kernel(c, levels, hg, dk, q_ref, f_ref, v_ref, g_ref, lb_ref, nw_ref, w_ref, mask_ref, s0_ref,
                o_ref, sout_ref, st_ref):
    ci = pl.program_id(2)

    @pl.when(ci == 0)
    def _():
        for h in range(hg):
            st_ref[h] = s0_ref[h].T

    n_lv = len(levels)
    row = lax.broadcasted_iota(jnp.int32, (c, dk), 0)
    w_all = w_ref[...]
    for h in range(hg):
        sl = slice(h * dk, (h + 1) * dk)
        zq, zf, v, zg = q_ref[:, sl], f_ref[:, sl], v_ref[:, sl], g_ref[:, sl]
        log_lb, log1m_lb, one_m_lb = lb_ref[h, 0:1, :], lb_ref[h, 1:2, :], lb_ref[h, 2:3, :]
        q = _silu(zq)
        log_sig = jnp.minimum(zf, 0.0) - jnp.log1p(jnp.exp(-jnp.abs(zf)))
        b = log1m_lb + log_sig
        g = jnp.maximum(log_lb, b) + jnp.log1p(jnp.exp(-jnp.abs(log_lb - b)))
        k = one_m_lb * jax.nn.sigmoid(-zf)
        g_hi, g_lo = _split2(g)
        e_all = jnp.exp(jnp.dot(w_all, g_hi, preferred_element_type=F32)
                        + jnp.dot(w_all, g_lo, preferred_element_type=F32))
        e_cum, e_tail = e_all[0:c], e_all[(n_lv + 1) * c:(n_lv + 2) * c]
        scores = mask_ref[n_lv] * _dot_nt(q, k)
        for li, m in enumerate(levels):
            e_l = e_all[(1 + li) * c:(2 + li) * c]
            second = (row & m) != 0
            scores = scores + mask_ref[li] * _dot_nt(jnp.where(second, q * e_l, 0.0), jnp.where(second, 0.0, k * e_l))
        st = st_ref[h]
        o = _dot(scores, v) + _dot_nt(q * e_cum, st)
        st_ref[h] = st * e_cum[c - 1:c, :] + _dot_tn(v, k * e_tail)
        o = o * lax.rsqrt(jnp.mean(o * o, -1, keepdims=True) + RMS_EPS) * nw_ref[...]
        o_ref[:, sl] = (o * _silu(zg)).astype(o_ref.dtype)

    @pl.when(ci == pl.num_programs(2) - 1)
    def _():
        for h in range(hg):
            sout_ref[h] = st_ref[h].T


def _gla_branch(proj, seg, grp, s0, lb3, norm_w):
    _, heads, dk, dv = s0.shape
    assert dk == dv
    hg = min(MIXER_HEADS_PER_STEP, heads)
    c, nc = grp.c, grp.nc
    w_all, masks, levels = _gla_consts(c)
    bw = hg * dk

    def col_spec(name):
        off = seg[name][0]
        assert off % bw == 0
        return pl.BlockSpec((c, bw), lambda b, hi, ci: (grp.row_blk(b, ci), off // bw + hi))

    const2 = lambda a: pl.BlockSpec(a.shape, lambda b, hi, ci: (0,) * a.ndim)
    state_spec = pl.BlockSpec((None, hg, dk, dv), lambda b, hi, ci: (b, hi, 0, 0))
    nw = norm_w.reshape(1, dv)
    o, s_new = pl.pallas_call(
        functools.partial(_gla_kernel, c, levels, hg, dk),
        grid=(grp.bsz, heads // hg, nc),
        in_specs=[col_spec("a_q"), col_spec("a_f"), col_spec("a_i"), col_spec("a_g"),
                  pl.BlockSpec((hg, 3, dk), lambda b, hi, ci: (hi, 0, 0)), const2(nw), const2(w_all), const2(masks),
                  state_spec],
        out_specs=[pl.BlockSpec((c, bw), lambda b, hi, ci: (b * nc + ci, hi)), state_spec],
        out_shape=[jax.ShapeDtypeStruct((grp.bsz * grp.t, heads * dv), BF16), jax.ShapeDtypeStruct(s0.shape, F32)],
        scratch_shapes=[pltpu.VMEM((hg, dv, dk), F32)],
        compiler_params=_params("parallel", "parallel", "arbitrary"),
        name="hgrn2",
    )(proj, proj, proj, proj, lb3, nw, w_all, masks, s0)
    return o, s_new


def _ret_kernel(c, hg, dk, dv, q_ref, k_ref, v_ref, g_ref, cos_ref, sin_ref, intra_ref, inner_ref, tail_ref,
                whole_ref, s0_ref, o_ref, sout_ref, s_ref):
    ci = pl.program_id(2)

    @pl.when(ci == 0)
    def _():
        s_ref[...] = s0_ref[...]

    cos, sin = cos_ref[...], sin_ref[...]
    for h in range(hg):
        sk, sv = slice(h * dk, (h + 1) * dk), slice(h * dv, (h + 1) * dv)
        zq, zk, v, zg = q_ref[:, sk], k_ref[:, sk], v_ref[:, sv], g_ref[:, sv]
        q = zq * cos + pltpu.roll(zq, dk // 2, 1) * sin
        k = (zk * cos + pltpu.roll(zk, dk // 2, 1) * sin) * (dk ** -0.5)
        s = s_ref[h]
        scores = _dot_nt(q, k) * intra_ref[h]
        o = _dot(scores, v) + _dot(q, s) * inner_ref[h]
        s_ref[h] = whole_ref[h] * s + _dot_tn(k * tail_ref[h], v)
        mu = jnp.mean(o, -1, keepdims=True)
        oc = o - mu
        var = jnp.mean(oc * oc, -1, keepdims=True)
        o_ref[:, sv] = (oc * lax.rsqrt(var + LN_EPS) * _silu(zg)).astype(o_ref.dtype)

    @pl.when(ci == pl.num_programs(2) - 1)
    def _():
        sout_ref[...] = s_ref[...]


def _ret_branch(proj, seg, grp, s0, pos0):
    _, heads, dk, dv = s0.shape
    hg = min(MIXER_HEADS_PER_STEP, heads)
    c, nc = grp.c, grp.nc
    half = dk // 2
    inv_freq = ROPE_BASE ** (-jnp.arange(half, dtype=F32) / half)
    ang = (pos0 + jnp.arange(grp.t)).astype(F32)[:, None] * inv_freq[None, :]
    cos = jnp.concatenate([jnp.cos(ang), jnp.cos(ang)], -1)
    sin = jnp.concatenate([-jnp.sin(ang), jnp.sin(ang)], -1)
    log_gamma = jnp.log1p(-jnp.exp2(-5.0 - jnp.arange(heads, dtype=F32)))
    idx = jnp.arange(c, dtype=F32)
    rel = idx[:, None] - idx[None, :]
    intra = jnp.where(rel >= 0, jnp.exp(jnp.maximum(rel, 0.0) * log_gamma[:, None, None]), 0.0)
    inner = jnp.broadcast_to(jnp.exp((idx + 1.0)[None, :] * log_gamma[:, None])[:, :, None], (heads, c, dv))
    tail = jnp.broadcast_to(jnp.exp((c - 1.0 - idx)[None, :] * log_gamma[:, None])[:, :, None], (heads, c, dk))
    whole = jnp.broadcast_to(jnp.exp(c * log_gamma)[:, None, None], (heads, 1, dv))

    def col_spec(name, width):
        off = seg[name][0]
        bw = hg * width
        assert off % bw == 0
        return pl.BlockSpec((c, bw), lambda b, hi, ci: (grp.row_blk(b, ci), off // bw + hi))

    tab_spec = pl.BlockSpec((c, dk), lambda b, hi, ci: (ci, 0))
    head_spec = lambda a: pl.BlockSpec((hg,) + a.shape[1:], lambda b, hi, ci: (hi, 0, 0))
    state_spec = pl.BlockSpec((None, hg, dk, dv), lambda b, hi, ci: (b, hi, 0, 0))
    o, s_new = pl.pallas_call(
        functools.partial(_ret_kernel, c, hg, dk, dv),
        grid=(grp.bsz, heads // hg, nc),
        in_specs=[col_spec("b_q", dk), col_spec("b_k", dk), col_spec("b_v", dv), col_spec("b_g", dv), tab_spec,
                  tab_spec, head_spec(intra), head_spec(inner), head_spec(tail), head_spec(whole), state_spec],
        out_specs=[pl.BlockSpec((c, hg * dv), lambda b, hi, ci: (b * nc + ci, hi)), state_spec],
        out_shape=[jax.ShapeDtypeStruct((grp.bsz * grp.t, heads * dv), BF16), jax.ShapeDtypeStruct(s0.shape, F32)],
        scratch_shapes=[pltpu.VMEM((hg, dk, dv), F32)],
        compiler_params=_params("parallel", "parallel", "arbitrary"),
        name="retention",
    )(proj, proj, proj, proj, cos, sin, intra, inner, tail, whole, s0)
    return o, s_new


CONV_HALO = 8


def _gdn_kernel(c, hg, dk, conv_width, q_ref, k_ref, v_ref, g_ref, small_ref, cwq_ref, cwk_ref, cwv_ref,
                csq_ref, csk_ref, csv_ref, ab_ref, nw_ref, tri_ref, s0_ref, o_ref, sout_ref, s_ref, ext_ref):
    ci = pl.program_id(2)
    r = hg * c
    bw = hg * dk

    @pl.when(ci == 0)
    def _():
        s_ref[...] = s0_ref[...]
        ext_ref[...] = jnp.zeros_like(ext_ref)
        for j, cs in enumerate((csq_ref, csk_ref, csv_ref)):
            ext_ref[j, CONV_HALO - (conv_width - 1):CONV_HALO, :] = cs[...]

    conv = []
    for j, (x_ref, cw_ref) in enumerate(((q_ref, cwq_ref), (k_ref, cwk_ref), (v_ref, cwv_ref))):
        x = x_ref[...]
        ext_ref[j, CONV_HALO:CONV_HALO + c, :] = x
        acc = x * cw_ref[conv_width - 1:conv_width, :]
        for w in range(conv_width - 1):
            shift = conv_width - 1 - w
            acc = acc + ext_ref[j, CONV_HALO - shift:CONV_HALO - shift + c, :] * cw_ref[w:w + 1, :]
        ext_ref[j, 0:CONV_HALO, :] = x[c - CONV_HALO:c, :]
        conv.append(_silu(acc))

    def l2n(x):
        return x * lax.rsqrt(jnp.sum(x * x, -1, keepdims=True) + RMS_EPS)

    def stack(parts):
        return jnp.concatenate(parts, axis=0)

    heads = range(hg)
    q_s = stack([l2n(conv[0][:, h * dk:(h + 1) * dk]) * (dk ** -0.5) for h in heads])
    k_s = stack([l2n(conv[1][:, h * dk:(h + 1) * dk]) for h in heads])
    v_s = stack([conv[2][:, h * dk:(h + 1) * dk] for h in heads])

    small = small_ref[...]
    g_all = -jnp.exp(ab_ref[0:1, :]) * _softplus(small + ab_ref[1:2, :])
    beta_all = jax.nn.sigmoid(small)
    g_b = stack([jnp.broadcast_to(g_all[:, h:h + 1], (c, dk)) for h in heads])
    beta_b = stack([jnp.broadcast_to(beta_all[:, hg + h:hg + h + 1], (c, dk)) for h in heads])

    tri = tri_ref[...]
    g_hi, g_lo = _split2(g_b)
    bcum = jnp.dot(tri, g_hi, preferred_element_type=F32) + jnp.dot(tri, g_lo, preferred_element_type=F32)
    b_hi = bcum.astype(BF16).astype(F32)
    b_lo = (bcum - b_hi).astype(BF16).astype(F32)
    lane = lax.broadcasted_iota(jnp.int32, (r, dk), 1)
    lhs = jnp.where(lane == 0, b_hi, jnp.where(lane == 1, b_lo, jnp.where(lane < 4, 1.0, 0.0)))
    rhs = jnp.where(lane < 2, 1.0, jnp.where(lane == 2, -b_hi, jnp.where(lane == 3, -b_lo, 0.0)))
    diff = _dot_nt(lhs, rhs)
    rr = lax.broadcasted_iota(jnp.int32, (r, r), 0)
    cc = lax.broadcasted_iota(jnp.int32, (r, r), 1)
    same = (rr // c) == (cc // c)
    incl = same & (rr >= cc)
    strict = same & (rr > cc)
    lmask = jnp.where(incl, jnp.exp(jnp.where(incl, diff, 0.0)), 0.0)

    kb = k_s * beta_b
    gram = _dot_nt(stack([kb, q_s]), k_s)
    a_low = jnp.where(strict, gram[0:r] * lmask, 0.0)
    attn = gram[r:2 * r] * lmask

    y = -a_low
    p = _dot(a_low, a_low)
    n_sq = int(round(math.log2(c))) - 1
    for j in range(n_sq):
        y = y + p + _dot(y, p)
        if j + 1 < n_sq:
            p = _dot(p, p)
    e_cum = jnp.exp(bcum)
    rhs_all = jnp.concatenate([v_s * beta_b, kb * e_cum], axis=1)
    rhs_hi, rhs_lo = _split2(rhs_all)
    y16 = y.astype(BF16)
    sol = (rhs_all + jnp.dot(y16, rhs_hi, preferred_element_type=F32)
           + jnp.dot(y16, rhs_lo, preferred_element_type=F32))
    u, w = sol[:, 0:dk], sol[:, dk:2 * dk]

    v_new = stack([u[h * c:(h + 1) * c] - _dot(w[h * c:(h + 1) * c], s_ref[h]) for h in heads])
    o_intra = _dot(attn, v_new)
    for h in heads:
        rows = slice(h * c, (h + 1) * c)
        s = s_ref[h]
        o = o_intra[rows] + _dot(q_s[rows] * e_cum[rows], s)
        b_last = bcum[(h + 1) * c - 1:(h + 1) * c, :]
        s_ref[h] = jnp.exp(b_last) * s + _dot_tn(k_s[rows] * jnp.exp(b_last - bcum[rows]), v_new[rows])
        o = o * lax.rsqrt(jnp.mean(o * o, -1, keepdims=True) + RMS_EPS) * nw_ref[...]
        o_ref[:, h * dk:(h + 1) * dk] = (o * _silu(g_ref[:, h * dk:(h + 1) * dk])).astype(o_ref.dtype)

    @pl.when(ci == pl.num_programs(2) - 1)
    def _():
        sout_ref[...] = s_ref[...]


def _gdn_branch(proj, small, seg, grp, s0, s_conv, conv_w, ab, norm_w):
    _, heads, dk, dv = s0.shape
    assert dk == dv
    hg = min(MIXER_HEADS_PER_STEP, heads)
    c, nc = grp.c, grp.nc
    assert c >= CONV_HALO
    conv_width = conv_w.shape[0]
    bw = hg * dk
    r = hg * c
    qk_cols = heads * dk
    off_qkv = seg["c_qkv"][0]
    idx = np.arange(r)
    tri = jnp.asarray(((idx[:, None] // c) == (idx[None, :] // c)) & (idx[:, None] >= idx[None, :]), BF16)

    def col_spec(off):
        assert off % bw == 0
        return pl.BlockSpec((c, bw), lambda b, hi, ci: (grp.row_blk(b, ci), off // bw + hi))

    def cw_spec(j):
        return pl.BlockSpec((conv_width, bw), lambda b, hi, ci: (0, j * (qk_cols // bw) + hi))

    def cs_spec(j):
        return pl.BlockSpec((None, conv_width - 1, bw), lambda b, hi, ci: (b, 0, j * (qk_cols // bw) + hi))

    const2 = lambda a: pl.BlockSpec(a.shape, lambda b, hi, ci: (0,) * a.ndim)
    state_spec = pl.BlockSpec((None, hg, dk, dv), lambda b, hi, ci: (b, hi, 0, 0))
    nw = norm_w.reshape(1, dv)
    o, s_new = pl.pallas_call(
        functools.partial(_gdn_kernel, c, hg, dk, conv_width),
        grid=(grp.bsz, heads // hg, nc),
        in_specs=[col_spec(off_qkv), col_spec(off_qkv + qk_cols), col_spec(off_qkv + 2 * qk_cols),
                  col_spec(seg["c_g"][0]),
                  pl.BlockSpec((c, V7X_LANES), lambda b, hi, ci: (grp.row_blk(b, ci), hi)),
                  cw_spec(0), cw_spec(1), cw_spec(2), cs_spec(0), cs_spec(1), cs_spec(2),
                  pl.BlockSpec((None, 2, V7X_LANES), lambda b, hi, ci: (hi, 0, 0)), const2(nw), const2(tri),
                  state_spec],
        out_specs=[pl.BlockSpec((c, bw), lambda b, hi, ci: (b * nc + ci, hi)), state_spec],
        out_shape=[jax.ShapeDtypeStruct((grp.bsz * grp.t, heads * dv), BF16), jax.ShapeDtypeStruct(s0.shape, F32)],
        scratch_shapes=[pltpu.VMEM((hg, dk, dv), F32), pltpu.VMEM((3, CONV_HALO + c, bw), F32)],
        compiler_params=_params("parallel", "parallel", "arbitrary"),
        name="gated_delta",
    )(proj, proj, proj, proj, small, conv_w, conv_w, conv_w, s_conv, s_conv, s_conv, ab, nw, tri, s0)
    return o, s_new


def kernel(x_prompt, x_sample, state_hgrn, state_ret, state_gdn, state_conv, lower_bounds, w_in, norm_a_w, conv_w,
           a_log, dt_bias, norm_c_w, w_br_a, w_br_b, w_br_c, w_out, ln1_g, ln1_b, ln2_g, ln2_b, ffn_w1, ffn_w3,
           ffn_w2, router, moe_w1, moe_w3, moe_w2):
    bp, tp, d = x_prompt.shape
    bs, ts, _ = x_sample.shape
    depth = state_hgrn.shape[0]
    _, _, a_heads, a_dk, a_dv = state_hgrn.shape
    _, _, b_heads, b_dk, b_dv = state_ret.shape
    _, _, c_heads, c_dk, c_dv = state_gdn.shape
    conv_width = conv_w.shape[1]
    alpha = (2 * depth) ** 0.25
    n_p = bp * tp

    a_qk, a_v = a_heads * a_dk, a_heads * a_dv
    b_qk, b_v = b_heads * b_dk, b_heads * b_dv
    c_qk, c_v = c_heads * c_dk, c_heads * c_dv
    sizes = [("a_q", a_qk), ("a_f", a_qk), ("a_i", a_v), ("a_g", a_v), ("b_q", b_qk), ("b_k", b_qk), ("b_v", b_v),
             ("b_g", b_v), ("c_qkv", 2 * c_qk + c_v), ("c_a", c_heads), ("c_b", c_heads), ("c_g", c_v),
             ("gate_a", d), ("gate_b", d), ("gate_c", d)]
    src_off, off = {}, 0
    for name, size in sizes:
        src_off[name] = off
        off += size
    seg, off = {}, 0
    for name, size in sizes:
        if name in ("c_a", "c_b"):
            continue
        seg[name] = (off, size)
        off += size
    lo, hi = src_off["c_a"], src_off["c_g"]
    w_main = jnp.concatenate([w_in[:, :, :lo], w_in[:, :, hi:]], axis=-1).astype(BF16)
    hg_c = min(MIXER_HEADS_PER_STEP, c_heads)
    n_grp = c_heads // hg_c

    def lane_groups(z):
        return z.reshape(z.shape[:-1] + (n_grp, hg_c))

    lane_pad = lambda z, used: jnp.pad(z, [(0, 0)] * (z.ndim - 1) + [(0, V7X_LANES - used)])
    w_small = lane_pad(jnp.concatenate([lane_groups(w_in[:, :, lo:lo + c_heads]),
                                        lane_groups(w_in[:, :, lo + c_heads:hi])], -1), 2 * hg_c)
    w_small = w_small.reshape(depth, d, n_grp * V7X_LANES).astype(BF16)
    ab = jnp.stack([lane_pad(lane_groups(a_log.astype(F32)), hg_c), lane_pad(lane_groups(dt_bias.astype(F32)), hg_c)],
                   axis=2)

    p = jax.nn.softmax(lower_bounds.astype(F32), axis=0)
    lb_all = jnp.cumsum(p, axis=0) - p[0:1]
    lbh = jnp.maximum(lb_all, 0.0).reshape(depth, a_heads, a_dk)
    lb3 = jnp.stack([jnp.log(lbh), jnp.log1p(-lbh), 1.0 - lbh], axis=2)

    x = jnp.concatenate([x_prompt.reshape(n_p, d), x_sample.reshape(bs * ts, d)], axis=0)
    x16 = x.astype(BF16)
    zeros = lambda b, s: jnp.zeros((b,) + s.shape[2:], F32)
    groups = (_Group(0, bp, tp), _Group(n_p, bs, ts))
    assert min(tp, ts) >= conv_width - 1
    off_qkv, n_qkv = seg["c_qkv"]
    new_states = [[] for _ in range(8)]
    for l in range(depth):
        proj = _matmul(x16, w_main[l], F32, name="in_proj")
        small = _matmul(x16, w_small[l], F32, name="in_proj_small")
        outs = []
        for gi, grp in enumerate(groups):
            if gi == 0:
                s_a, s_b, s_c, s_cv = (zeros(bp, s) for s in (state_hgrn, state_ret, state_gdn, state_conv))
                pos0 = 0
            else:
                s_a, s_b, s_c, s_cv, pos0 = state_hgrn[l], state_ret[l], state_gdn[l], state_conv[l], PAST_LEN
            o_a, s_a = _gla_branch(proj, seg, grp, s_a, lb3[l], norm_a_w[l])
            o_b, s_b = _ret_branch(proj, seg, grp, s_b, pos0)
            o_c, s_c = _gdn_branch(proj, small, seg, grp, s_c, s_cv, conv_w[l], ab[l], norm_c_w[l])
            rows = proj[grp.row0:grp.row0 + grp.bsz * grp.t].reshape(grp.bsz, grp.t, -1)
            s_cv = rows[:, grp.t - (conv_width - 1):, off_qkv:off_qkv + n_qkv]
            outs.append((o_a, o_b, o_c))
            for i, s in enumerate((s_a, s_b, s_c, s_cv)):
                new_states[4 * gi + i].append(s)
        o_a, o_b, o_c = (jnp.concatenate([outs[0][i], outs[1][i]], axis=0) for i in range(3))
        merged = _merge(o_a, o_b, o_c, w_br_a[l].astype(BF16), w_br_b[l].astype(BF16), w_br_c[l].astype(BF16),
                        proj, [seg[n][0] for n in ("gate_a", "gate_b", "gate_c")])
        y = _matmul(merged, w_out[l].astype(BF16), F32, name="out_proj")
        x, x16 = _add_ln(x, [y], ln1_g[l], ln1_b[l], alpha)
        if l % 2 == 0:
            j = l // 2
            f = [_dense_ffn(x16, ffn_w1[j:j + 1].astype(BF16), ffn_w3[j:j + 1].astype(BF16),
                            ffn_w2[j:j + 1].astype(BF16))]
        else:
            j = l // 2
            f = _moe_ffn(x, x16, router[j], moe_w1[j].astype(BF16), moe_w3[j].astype(BF16), moe_w2[j].astype(BF16))
        x, x16 = _add_ln(x, f, ln2_g[l], ln2_b[l], alpha)

    y_prompt = x[:n_p].reshape(bp, tp, d)
    y_sample = x[n_p:].reshape(bs, ts, d)
    return (y_prompt, y_sample) + tuple(jnp.stack(s) for s in new_states)
```

```python
import functools
import math

import jax
import jax.numpy as jnp
import numpy as np
from jax import lax
from jax.experimental import pallas as pl
from jax.experimental.pallas import tpu as pltpu

F32 = jnp.float32
BF16 = jnp.bfloat16

CHUNK = 64
TOP_K = 2
PAST_LEN = 2048
ROPE_BASE = 10000.0
LN_EPS = 1e-5
RMS_EPS = 1e-6

V7X_LANES = 128
V7X_VMEM_LIMIT_BYTES = 56 * 1024 * 1024


def _tile(n, target, align=V7X_LANES):
    if n <= target:
        return n
    t = (target // align) * align
    while t >= align:
        if n % t == 0:
            return t
        t -= align
    raise ValueError(f"no {align}-aligned tile <= {target} divides {n}")


def _params(*sem):
    return pltpu.CompilerParams(dimension_semantics=sem, vmem_limit_bytes=V7X_VMEM_LIMIT_BYTES)


def _mm_kernel(x_ref, w_ref, o_ref):
    o_ref[...] = jnp.dot(x_ref[...], w_ref[...], preferred_element_type=F32).astype(o_ref.dtype)


def _matmul(x, w, out_dtype, *, bm=1024, bn=1024, name="matmul"):
    m, k = x.shape
    n = w.shape[1]
    bm, bn = _tile(m, bm, 8), _tile(n, bn)
    return pl.pallas_call(
        _mm_kernel,
        grid=(m // bm, n // bn),
        in_specs=[pl.BlockSpec((bm, k), lambda i, j: (i, 0)), pl.BlockSpec((k, bn), lambda i, j: (0, j))],
        out_specs=pl.BlockSpec((bm, bn), lambda i, j: (i, j)),
        out_shape=jax.ShapeDtypeStruct((m, n), out_dtype),
        compiler_params=_params("parallel", "parallel"),
        name=name,
    )(x, w)


def _merge_kernel(oa_ref, ob_ref, oc_ref, wa_ref, wb_ref, wc_ref, ga_ref, gb_ref, gc_ref, o_ref):
    acc = jax.nn.sigmoid(ga_ref[...]) * jnp.dot(oa_ref[...], wa_ref[...], preferred_element_type=F32)
    acc = acc + jax.nn.sigmoid(gb_ref[...]) * jnp.dot(ob_ref[...], wb_ref[...], preferred_element_type=F32)
    acc = acc + jax.nn.sigmoid(gc_ref[...]) * jnp.dot(oc_ref[...], wc_ref[...], preferred_element_type=F32)
    o_ref[...] = acc.astype(o_ref.dtype)


def _merge(o_a, o_b, o_c, w_a, w_b, w_c, proj, gate_offs, *, bm=512, bn=1024):
    m = o_a.shape[0]
    d = w_a.shape[1]
    bm, bn = _tile(m, bm, 8), _tile(d, bn)
    assert all(off % bn == 0 for off in gate_offs)
    ga, gb, gc = (off // bn for off in gate_offs)

    def o_spec(o):
        return pl.BlockSpec((bm, o.shape[1]), lambda i, j: (i, 0))

    def w_spec(w):
        return pl.BlockSpec((w.shape[0], bn), lambda i, j: (0, j))

    def g_spec(blk):
        return pl.BlockSpec((bm, bn), lambda i, j: (i, blk + j))

    return pl.pallas_call(
        _merge_kernel,
        grid=(m // bm, d // bn),
        in_specs=[o_spec(o_a), o_spec(o_b), o_spec(o_c), w_spec(w_a), w_spec(w_b), w_spec(w_c),
                  g_spec(ga), g_spec(gb), g_spec(gc)],
        out_specs=pl.BlockSpec((bm, bn), lambda i, j: (i, j)),
        out_shape=jax.ShapeDtypeStruct((m, d), BF16),
        compiler_params=_params("parallel", "parallel"),
        name="merge",
    )(o_a, o_b, o_c, w_a, w_b, w_c, proj, proj, proj)


def _add_ln_kernel(alpha, n_y, res_ref, *refs):
    y_refs, (g_ref, b_ref, o32_ref, o16_ref) = refs[:n_y], refs[n_y:]
    x = alpha * res_ref[...]
    y = y_refs[0][...]
    for r in y_refs[1:]:
        y = y + r[...]
    x = x + y
    mu = jnp.mean(x, -1, keepdims=True)
    xc = x - mu
    var = jnp.mean(xc * xc, -1, keepdims=True)
    out = xc * lax.rsqrt(var + LN_EPS) * g_ref[...] + b_ref[...]
    o32_ref[...] = out
    o16_ref[...] = out.astype(BF16)


def _add_ln(res, ys, g, b, alpha, *, bm=256):
    m, d = res.shape
    bm = _tile(m, bm, 8)
    row = pl.BlockSpec((bm, d), lambda i: (i, 0))
    vec = pl.BlockSpec((1, d), lambda i: (0, 0))
    return pl.pallas_call(
        functools.partial(_add_ln_kernel, alpha, len(ys)),
        grid=(m // bm,),
        in_specs=[row] * (1 + len(ys)) + [vec, vec],
        out_specs=[row, row],
        out_shape=[jax.ShapeDtypeStruct((m, d), F32), jax.ShapeDtypeStruct((m, d), BF16)],
        compiler_params=_params("parallel"),
        name="add_ln",
    )(res, *ys, g.reshape(1, d), b.reshape(1, d))


def _swiglu_up_kernel(has_gate, te_ref, nu_ref, x_ref, w1_ref, w3_ref, *refs):
    o_ref = refs[-1]

    @pl.when(pl.program_id(1) < nu_ref[0])
    def _():
        x = x_ref[...]
        a = jnp.dot(x, w1_ref[...], preferred_element_type=F32)
        b = jnp.dot(x, w3_ref[...], preferred_element_type=F32)
        hid = (a * jax.nn.sigmoid(a)) * b
        if has_gate:
            hid = hid * refs[0][...]
        o_ref[...] = hid.astype(o_ref.dtype)

    @pl.when(pl.program_id(1) >= nu_ref[0])
    def _():
        o_ref[...] = jnp.zeros_like(o_ref)


def _swiglu_up(x, w1, w3, row_gate, tile_expert, n_used, *, bm, bn=1024):
    m, k = x.shape
    n = w1.shape[-1]
    bn = _tile(n, bn)
    assert m % bm == 0

    def row_blk(j, i, te, nu):
        return jnp.minimum(i, nu[0] - 1)

    in_specs = [pl.BlockSpec((bm, k), lambda j, i, te, nu: (row_blk(j, i, te, nu), 0)),
                pl.BlockSpec((None, k, bn), lambda j, i, te, nu: (te[i], 0, j)),
                pl.BlockSpec((None, k, bn), lambda j, i, te, nu: (te[i], 0, j))]
    args = [x, w1, w3]
    if row_gate is not None:
        in_specs.append(pl.BlockSpec((bm, 1), lambda j, i, te, nu: (row_blk(j, i, te, nu), 0)))
        args.append(row_gate)
    return pl.pallas_call(
        functools.partial(_swiglu_up_kernel, row_gate is not None),
        grid_spec=pltpu.PrefetchScalarGridSpec(
            num_scalar_prefetch=2,
            grid=(n // bn, m // bm),
            in_specs=in_specs,
            out_specs=pl.BlockSpec((bm, bn), lambda j, i, te, nu: (i, j)),
        ),
        out_shape=jax.ShapeDtypeStruct((m, n), BF16),
        compiler_params=_params("arbitrary", "arbitrary"),
        name="swiglu_up",
    )(tile_expert, n_used, *args)


def _mm_acc_kernel(te_ref, nu_ref, x_ref, w_ref, o_ref, acc_ref):
    k = pl.program_id(2)

    @pl.when(pl.program_id(1) < nu_ref[0])
    def _():
        @pl.when(k == 0)
        def _():
            acc_ref[...] = jnp.zeros_like(acc_ref)

        acc_ref[...] += jnp.dot(x_ref[...], w_ref[...], preferred_element_type=F32)

        @pl.when(k == pl.num_programs(2) - 1)
        def _():
            o_ref[...] = acc_ref[...].astype(o_ref.dtype)

    @pl.when(pl.program_id(1) >= nu_ref[0])
    def _():
        o_ref[...] = jnp.zeros_like(o_ref)


def _grouped_matmul(x, w, tile_expert, n_used, out_dtype, *, bm, bn=2048, bk=2048):
    m, kdim = x.shape
    n = w.shape[-1]
    bn, bk = _tile(n, bn), _tile(kdim, bk)
    assert m % bm == 0

    def row_blk(i, nu):
        return jnp.minimum(i, nu[0] - 1)

    return pl.pallas_call(
        _mm_acc_kernel,
        grid_spec=pltpu.PrefetchScalarGridSpec(
            num_scalar_prefetch=2,
            grid=(n // bn, m // bm, kdim // bk),
            in_specs=[pl.BlockSpec((bm, bk), lambda j, i, kk, te, nu: (row_blk(i, nu), kk)),
                      pl.BlockSpec((None, bk, bn), lambda j, i, kk, te, nu: (te[i], kk, j))],
            out_specs=pl.BlockSpec((bm, bn), lambda j, i, kk, te, nu: (i, j)),
            scratch_shapes=[pltpu.VMEM((bm, bn), F32)],
        ),
        out_shape=jax.ShapeDtypeStruct((m, n), out_dtype),
        compiler_params=_params("arbitrary", "arbitrary", "arbitrary"),
        name="grouped_matmul",
    )(tile_expert, n_used, x, w)


def _router_kernel(n_experts, x_ref, w_ref, idx_ref, gate_ref):
    x = x_ref[...]
    w = w_ref[...]
    x_hi = x.astype(BF16)
    x_lo = (x - x_hi.astype(F32)).astype(BF16)
    w_hi = w.astype(BF16)
    w_lo = (w - w_hi.astype(F32)).astype(BF16)
    logits = (jnp.dot(x_hi, w_lo, preferred_element_type=F32) + jnp.dot(x_lo, w_hi, preferred_element_type=F32)
              + jnp.dot(x_hi, w_hi, preferred_element_type=F32))
    lane = lax.broadcasted_iota(jnp.int32, logits.shape, 1)
    neg = jnp.float32(-jnp.inf)
    l1 = jnp.where(lane < n_experts, logits, neg)
    m1 = jnp.max(l1, -1, keepdims=True)
    i1 = jnp.min(jnp.where(l1 == m1, lane, V7X_LANES), -1, keepdims=True)
    l2 = jnp.where(lane == i1, neg, l1)
    m2 = jnp.max(l2, -1, keepdims=True)
    i2 = jnp.min(jnp.where(l2 == m2, lane, V7X_LANES), -1, keepdims=True)
    e2 = jnp.exp(m2 - m1)
    den = 1.0 + e2
    idx_ref[...] = jnp.where(lane == 0, i1, jnp.where(lane == 1, i2, 0))
    gate_ref[...] = jnp.where(lane == 0, 1.0 / den, jnp.where(lane == 1, e2 / den, 0.0))


def _router(x, w_router, *, bm=512):
    m, d = x.shape
    n_experts = w_router.shape[1]
    bm = _tile(m, bm, 8)
    w_pad = jnp.pad(w_router, ((0, 0), (0, V7X_LANES - n_experts)))
    blk = pl.BlockSpec((bm, V7X_LANES), lambda i: (i, 0))
    idx, gate = pl.pallas_call(
        functools.partial(_router_kernel, n_experts),
        grid=(m // bm,),
        in_specs=[pl.BlockSpec((bm, d), lambda i: (i, 0)), pl.BlockSpec((d, V7X_LANES), lambda i: (0, 0))],
        out_specs=[blk, blk],
        out_shape=[jax.ShapeDtypeStruct((m, V7X_LANES), jnp.int32), jax.ShapeDtypeStruct((m, V7X_LANES), F32)],
        compiler_params=_params("parallel"),
        name="router",
    )(x, w_pad)
    return idx[:, :TOP_K], gate[:, :TOP_K]


def _gather_kernel(bm, idx_ref, src_ref, o_ref, buf_ref, sem):
    base = pl.program_id(0) * bm

    def row_copy(r, src_row):
        return pltpu.make_async_copy(src_ref.at[pl.ds(src_row, 1)], buf_ref.at[pl.ds(r, 1)], sem)

    def start(r, carry):
        row_copy(r, idx_ref[base + r]).start()
        return carry

    def wait(r, carry):
        row_copy(r, 0).wait()
        return carry

    lax.fori_loop(0, bm, start, 0)
    lax.fori_loop(0, bm, wait, 0)
    o_ref[...] = buf_ref[...].astype(o_ref.dtype)


def _gather_rows(src, idx, out_dtype, *, bm=256):
    n = idx.shape[0]
    d = src.shape[1]
    bm = _tile(n, bm, 8)
    return pl.pallas_call(
        functools.partial(_gather_kernel, bm),
        grid_spec=pltpu.PrefetchScalarGridSpec(
            num_scalar_prefetch=1,
            grid=(n // bm,),
            in_specs=[pl.BlockSpec(memory_space=pl.ANY)],
            out_specs=pl.BlockSpec((bm, d), lambda i, idx: (i, 0)),
            scratch_shapes=[pltpu.VMEM((bm, d), src.dtype), pltpu.SemaphoreType.DMA(())],
        ),
        out_shape=jax.ShapeDtypeStruct((n, d), out_dtype),
        compiler_params=_params("arbitrary"),
        name="gather_rows",
    )(idx, src)


def _dense_ffn(x16, w1, w3, w2, *, bm=512):
    m = x16.shape[0]
    bm = _tile(m, bm, 8)
    tile_expert = jnp.zeros((m // bm,), jnp.int32)
    n_used = jnp.full((1,), m // bm, jnp.int32)
    hid = _swiglu_up(x16, w1, w3, None, tile_expert, n_used, bm=bm)
    return _grouped_matmul(hid, w2, tile_expert, n_used, F32, bm=bm)


def _moe_ffn(x32, x16, w_router, w1, w3, w2, *, bm=512):
    t = x32.shape[0]
    n_experts = w_router.shape[1]
    bm = _tile(t, bm, 8)
    top_idx, top_w = _router(x32, w_router)
    flat_e = top_idx.reshape(-1)
    onehot = (flat_e[:, None] == jnp.arange(n_experts, dtype=jnp.int32)[None, :]).astype(jnp.int32)
    rank = jnp.sum((jnp.cumsum(onehot, axis=0) - onehot) * onehot, axis=1)
    counts = jnp.sum(onehot, axis=0)
    padded = ((counts + bm - 1) // bm) * bm
    ends = jnp.cumsum(padded)
    pos = (ends - padded)[flat_e] + rank
    n_rows = TOP_K * t + n_experts * bm
    n_tiles = n_rows // bm
    n_used = (ends[-1] // bm).astype(jnp.int32).reshape(1)
    tile_start = jnp.minimum(jnp.arange(n_tiles, dtype=jnp.int32), n_used[0] - 1) * bm
    tile_expert = jnp.minimum(jnp.searchsorted(ends, tile_start, side="right"), n_experts - 1).astype(jnp.int32)
    token_of_row = jnp.arange(TOP_K * t, dtype=jnp.int32) // TOP_K
    slot_token = jnp.zeros((n_rows,), jnp.int32).at[pos].set(token_of_row)
    slot_gate = jnp.zeros((n_rows,), F32).at[pos].set(top_w.reshape(-1))

    x_sorted = _gather_rows(x32, slot_token, BF16)
    hid = _swiglu_up(x_sorted, w1, w3, slot_gate.reshape(n_rows, 1), tile_expert, n_used, bm=bm)
    out_sorted = _grouped_matmul(hid, w2, tile_expert, n_used, F32, bm=bm)
    pos2 = pos.reshape(t, TOP_K).astype(jnp.int32)
    return [_gather_rows(out_sorted, pos2[:, k], F32) for k in range(TOP_K)]


MIXER_HEADS_PER_STEP = 4
GLA_HEADS_PER_STEP = 8
RET_HEADS_PER_STEP = 8


def _silu(x):
    return x * jax.nn.sigmoid(x)


def _dot(a, b):
    return jnp.dot(a.astype(BF16), b.astype(BF16), preferred_element_type=F32)


def _dot_nt(a, b):
    return lax.dot_general(a.astype(BF16), b.astype(BF16), (((1,), (1,)), ((), ())), preferred_element_type=F32)


def _dot_tn(a, b):
    return lax.dot_general(a.astype(BF16), b.astype(BF16), (((0,), (0,)), ((), ())), preferred_element_type=F32)


def _split2(x):
    hi = x.astype(BF16)
    lo = (x - hi.astype(F32)).astype(BF16)
    return hi, lo


def _softplus(x):
    return jnp.maximum(x, 0.0) + jnp.log1p(jnp.exp(-jnp.abs(x)))


def _interleave(gens):
    while gens:
        alive = []
        for gen in gens:
            try:
                next(gen)
                alive.append(gen)
            except StopIteration:
                pass
        gens = alive


class _Group:
    def __init__(self, row0, bsz, t):
        self.row0, self.bsz, self.t = row0, bsz, t
        self.c = math.gcd(t, CHUNK)
        self.nc = t // self.c
        assert row0 % self.c == 0
        self.blk0 = row0 // self.c

    def row_blk(self, b, ci):
        return self.blk0 + b * self.nc + ci


def _gla_consts(c):
    t = np.arange(c)[:, None]
    u = np.arange(c)[None, :]
    mats, masks, levels = [u <= t], [], []
    m = c // 2
    while m >= 1:
        same = (t // (2 * m)) == (u // (2 * m))
        t2, u2 = (t % (2 * m)) >= m, (u % (2 * m)) >= m
        mats.append(same & ((t2 & u2 & (u <= t)) | (~t2 & ~u2 & (u > t))))
        masks.append(same & t2 & ~u2)
        levels.append(m)
        m //= 2
    mats.append(u > t)
    masks.append(t == u)
    return (jnp.asarray(np.concatenate(mats, 0), BF16), jnp.asarray(np.stack(masks), F32), tuple(levels))


def _gla_kernel(c, levels, hg, dk, q_ref, f_ref, v_ref, g_ref, lb_ref, nw_ref, w_ref, mask_ref, s0_ref,
                o_ref, sout_ref, st_ref):
    ci = pl.program_id(2)

    @pl.when(ci == 0)
    def _():
        for h in range(hg):
            st_ref[h] = s0_ref[h].T

    n_lv = len(levels)
    row = lax.broadcasted_iota(jnp.int32, (c, dk), 0)
    w_all = w_ref[...]

    def head(h):
        sl = slice(h * dk, (h + 1) * dk)
        zq, zf, v, zg = q_ref[:, sl], f_ref[:, sl], v_ref[:, sl], g_ref[:, sl]
        log_lb, log1m_lb, one_m_lb = lb_ref[h, 0:1, :], lb_ref[h, 1:2, :], lb_ref[h, 2:3, :]
        q = _silu(zq)
        log_sig = jnp.minimum(zf, 0.0) - jnp.log1p(jnp.exp(-jnp.abs(zf)))
        b = log1m_lb + log_sig
        g = jnp.maximum(log_lb, b) + jnp.log1p(jnp.exp(-jnp.abs(log_lb - b)))
        k = one_m_lb * jax.nn.sigmoid(-zf)
        g_hi, g_lo = _split2(g)
        x_all = jnp.dot(w_all, g_hi, preferred_element_type=F32) + jnp.dot(w_all, g_lo, preferred_element_type=F32)
        yield
        e_all = jnp.exp(x_all)
        e_cum, e_tail = e_all[0:c], e_all[(n_lv + 1) * c:(n_lv + 2) * c]
        prods = [_dot_nt(q, k)]
        for li, m in enumerate(levels):
            e_l = e_all[(1 + li) * c:(2 + li) * c]
            second = (row & m) != 0
            prods.append(_dot_nt(jnp.where(second, q * e_l, 0.0), jnp.where(second, 0.0, k * e_l)))
        yield
        scores = mask_ref[n_lv] * prods[0]
        for li in range(n_lv):
            scores = scores + mask_ref[li] * prods[1 + li]
        st = st_ref[h]
        o = _dot(scores, v) + _dot_nt(q * e_cum, st)
        st_new = st * e_cum[c - 1:c, :] + _dot_tn(v, k * e_tail)
        yield
        st_ref[h] = st_new
        o = o * lax.rsqrt(jnp.mean(o * o, -1, keepdims=True) + RMS_EPS) * nw_ref[...]
        o_ref[:, sl] = (o * _silu(zg)).astype(o_ref.dtype)

    _interleave([head(h) for h in range(hg)])

    @pl.when(ci == pl.num_programs(2) - 1)
    def _():
        for h in range(hg):
            sout_ref[h] = st_ref[h].T


def _gla_branch(proj, seg, grp, s0, lb3, norm_w):
    _, heads, dk, dv = s0.shape
    assert dk == dv
    hg = min(GLA_HEADS_PER_STEP, heads)
    c, nc = grp.c, grp.nc
    w_all, masks, levels = _gla_consts(c)
    bw = hg * dk

    def col_spec(name):
        off = seg[name][0]
        assert off % bw == 0
        return pl.BlockSpec((c, bw), lambda b, hi, ci: (grp.row_blk(b, ci), off // bw + hi))

    const2 = lambda a: pl.BlockSpec(a.shape, lambda b, hi, ci: (0,) * a.ndim)
    state_spec = pl.BlockSpec((None, hg, dk, dv), lambda b, hi, ci: (b, hi, 0, 0))
    nw = norm_w.reshape(1, dv)
    o, s_new = pl.pallas_call(
        functools.partial(_gla_kernel, c, levels, hg, dk),
        grid=(grp.bsz, heads // hg, nc),
        in_specs=[col_spec("a_q"), col_spec("a_f"), col_spec("a_i"), col_spec("a_g"),
                  pl.BlockSpec((hg, 3, dk), lambda b, hi, ci: (hi, 0, 0)), const2(nw), const2(w_all), const2(masks),
                  state_spec],
        out_specs=[pl.BlockSpec((c, bw), lambda b, hi, ci: (b * nc + ci, hi)), state_spec],
        out_shape=[jax.ShapeDtypeStruct((grp.bsz * grp.t, heads * dv), BF16), jax.ShapeDtypeStruct(s0.shape, F32)],
        scratch_shapes=[pltpu.VMEM((hg, dv, dk), F32)],
        compiler_params=_params("parallel", "parallel", "arbitrary"),
        name="hgrn2",
    )(proj, proj, proj, proj, lb3, nw, w_all, masks, s0)
    return o, s_new


def _ret_kernel(c, hg, dk, dv, q_ref, k_ref, v_ref, g_ref, cos_ref, sin_ref, intra_ref, inner_ref, tail_ref,
                whole_ref, s0_ref, o_ref, sout_ref, s_ref):
    ci = pl.program_id(2)

    @pl.when(ci == 0)
    def _():
        s_ref[...] = s0_ref[...]

    cos, sin = cos_ref[...], sin_ref[...]
    def head(h):
        sk, sv = slice(h * dk, (h + 1) * dk), slice(h * dv, (h + 1) * dv)
        zq, zk, v, zg = q_ref[:, sk], k_ref[:, sk], v_ref[:, sv], g_ref[:, sv]
        q = zq * cos + pltpu.roll(zq, dk // 2, 1) * sin
        k = (zk * cos + pltpu.roll(zk, dk // 2, 1) * sin) * (dk ** -0.5)
        s = s_ref[h]
        qk = _dot_nt(q, k)
        inter = _dot(q, s)
        s_new = whole_ref[h] * s + _dot_tn(k * tail_ref[h], v)
        yield
        o = _dot(qk * intra_ref[h], v) + inter * inner_ref[h]
        yield
        s_ref[h] = s_new
        mu = jnp.mean(o, -1, keepdims=True)
        oc = o - mu
        var = jnp.mean(oc * oc, -1, keepdims=True)
        o_ref[:, sv] = (oc * lax.rsqrt(var + LN_EPS) * _silu(zg)).astype(o_ref.dtype)

    _interleave([head(h) for h in range(hg)])

    @pl.when(ci == pl.num_programs(2) - 1)
    def _():
        sout_ref[...] = s_ref[...]


def _ret_branch(proj, seg, grp, s0, pos0):
    _, heads, dk, dv = s0.shape
    hg = min(RET_HEADS_PER_STEP, heads)
    c, nc = grp.c, grp.nc
    half = dk // 2
    inv_freq = ROPE_BASE ** (-jnp.arange(half, dtype=F32) / half)
    ang = (pos0 + jnp.arange(grp.t)).astype(F32)[:, None] * inv_freq[None, :]
    cos = jnp.concatenate([jnp.cos(ang), jnp.cos(ang)], -1)
    sin = jnp.concatenate([-jnp.sin(ang), jnp.sin(ang)], -1)
    log_gamma = jnp.log1p(-jnp.exp2(-5.0 - jnp.arange(heads, dtype=F32)))
    idx = jnp.arange(c, dtype=F32)
    rel = idx[:, None] - idx[None, :]
    intra = jnp.where(rel >= 0, jnp.exp(jnp.maximum(rel, 0.0) * log_gamma[:, None, None]), 0.0)
    inner = jnp.broadcast_to(jnp.exp((idx + 1.0)[None, :] * log_gamma[:, None])[:, :, None], (heads, c, dv))
    tail = jnp.broadcast_to(jnp.exp((c - 1.0 - idx)[None, :] * log_gamma[:, None])[:, :, None], (heads, c, dk))
    whole = jnp.broadcast_to(jnp.exp(c * log_gamma)[:, None, None], (heads, 1, dv))

    def col_spec(name, width):
        off = seg[name][0]
        bw = hg * width
        assert off % bw == 0
        return pl.BlockSpec((c, bw), lambda b, hi, ci: (grp.row_blk(b, ci), off // bw + hi))

    tab_spec = pl.BlockSpec((c, dk), lambda b, hi, ci: (ci, 0))
    head_spec = lambda a: pl.BlockSpec((hg,) + a.shape[1:], lambda b, hi, ci: (hi, 0, 0))
    state_spec = pl.BlockSpec((None, hg, dk, dv), lambda b, hi, ci: (b, hi, 0, 0))
    o, s_new = pl.pallas_call(
        functools.partial(_ret_kernel, c, hg, dk, dv),
        grid=(grp.bsz, heads // hg, nc),
        in_specs=[col_spec("b_q", dk), col_spec("b_k", dk), col_spec("b_v", dv), col_spec("b_g", dv), tab_spec,
                  tab_spec, head_spec(intra), head_spec(inner), head_spec(tail), head_spec(whole), state_spec],
        out_specs=[pl.BlockSpec((c, hg * dv), lambda b, hi, ci: (b * nc + ci, hi)), state_spec],
        out_shape=[jax.ShapeDtypeStruct((grp.bsz * grp.t, heads * dv), BF16), jax.ShapeDtypeStruct(s0.shape, F32)],
        scratch_shapes=[pltpu.VMEM((hg, dk, dv), F32)],
        compiler_params=_params("parallel", "parallel", "arbitrary"),
        name="retention",
    )(proj, proj, proj, proj, cos, sin, intra, inner, tail, whole, s0)
    return o, s_new


CONV_HALO = 8
GDN_GROUPS_PER_STEP = 2


def _gdn_kernel(c, hg, ng, dk, conv_width, q_ref, k_ref, v_ref, g_ref, small_ref, cwq_ref, cwk_ref, cwv_ref,
                csq_ref, csk_ref, csv_ref, ab_ref, nw_ref, tri_ref, s0_ref, o_ref, sout_ref, s_ref, ext_ref):
    ci = pl.program_id(2)
    r = hg * c

    @pl.when(ci == 0)
    def _():
        s_ref[...] = s0_ref[...]
        ext_ref[...] = jnp.zeros_like(ext_ref)
        for j, cs in enumerate((csq_ref, csk_ref, csv_ref)):
            ext_ref[j, CONV_HALO - (conv_width - 1):CONV_HALO, :] = cs[...]

    conv = []
    for j, (x_ref, cw_ref) in enumerate(((q_ref, cwq_ref), (k_ref, cwk_ref), (v_ref, cwv_ref))):
        x = x_ref[...]
        ext_ref[j, CONV_HALO:CONV_HALO + c, :] = x
        acc = x * cw_ref[conv_width - 1:conv_width, :]
        for w in range(conv_width - 1):
            shift = conv_width - 1 - w
            acc = acc + ext_ref[j, CONV_HALO - shift:CONV_HALO - shift + c, :] * cw_ref[w:w + 1, :]
        ext_ref[j, 0:CONV_HALO, :] = x[c - CONV_HALO:c, :]
        conv.append(_silu(acc))

    def l2n(x):
        return x * lax.rsqrt(jnp.sum(x * x, -1, keepdims=True) + RMS_EPS)

    def stack(parts):
        return jnp.concatenate(parts, axis=0)

    tri = tri_ref[...]
    lane = lax.broadcasted_iota(jnp.int32, (r, dk), 1)
    rr = lax.broadcasted_iota(jnp.int32, (r, r), 0)
    cc = lax.broadcasted_iota(jnp.int32, (r, r), 1)
    same = (rr // c) == (cc // c)
    incl = same & (rr >= cc)
    strict = same & (rr > cc)
    n_sq = int(round(math.log2(c))) - 1
    heads = range(hg)

    def group(gi):
        col = lambda h: slice((gi * hg + h) * dk, (gi * hg + h + 1) * dk)
        q_s = stack([l2n(conv[0][:, col(h)]) * (dk ** -0.5) for h in heads])
        k_s = stack([l2n(conv[1][:, col(h)]) for h in heads])
        v_s = stack([conv[2][:, col(h)] for h in heads])
        small = small_ref[:, gi * V7X_LANES:(gi + 1) * V7X_LANES]
        g_all = -jnp.exp(ab_ref[gi, 0:1, :]) * _softplus(small + ab_ref[gi, 1:2, :])
        beta_all = jax.nn.sigmoid(small)
        g_b = stack([jnp.broadcast_to(g_all[:, h:h + 1], (c, dk)) for h in heads])
        beta_b = stack([jnp.broadcast_to(beta_all[:, hg + h:hg + h + 1], (c, dk)) for h in heads])
        g_hi, g_lo = _split2(g_b)
        bcum = jnp.dot(tri, g_hi, preferred_element_type=F32) + jnp.dot(tri, g_lo, preferred_element_type=F32)
        kb = k_s * beta_b
        gram = _dot_nt(stack([kb, q_s]), k_s)
        yield
        b_hi = bcum.astype(BF16).astype(F32)
        b_lo = (bcum - b_hi).astype(BF16).astype(F32)
        lhs = jnp.where(lane == 0, b_hi, jnp.where(lane == 1, b_lo, jnp.where(lane < 4, 1.0, 0.0)))
        rhs = jnp.where(lane < 2, 1.0, jnp.where(lane == 2, -b_hi, jnp.where(lane == 3, -b_lo, 0.0)))
        diff = _dot_nt(lhs, rhs)
        yield
        lmask = jnp.where(incl, jnp.exp(jnp.where(incl, diff, 0.0)), 0.0)
        a_low = jnp.where(strict, gram[0:r] * lmask, 0.0)
        attn = gram[r:2 * r] * lmask
        y = -a_low
        p = _dot(a_low, a_low)
        yield
        for j in range(n_sq):
            yp = _dot(y, p)
            p_next = _dot(p, p) if j + 1 < n_sq else None
            yield
            y = y + p + yp
            p = p_next
        e_cum = jnp.exp(bcum)
        rhs_all = jnp.concatenate([v_s * beta_b, kb * e_cum], axis=1)
        rhs_hi, rhs_lo = _split2(rhs_all)
        y16 = y.astype(BF16)
        sol = (rhs_all + jnp.dot(y16, rhs_hi, preferred_element_type=F32)
               + jnp.dot(y16, rhs_lo, preferred_element_type=F32))
        yield
        u, w = sol[:, 0:dk], sol[:, dk:2 * dk]
        states = [s_ref[gi * hg + h] for h in heads]
        v_new = stack([u[h * c:(h + 1) * c] - _dot(w[h * c:(h + 1) * c], states[h]) for h in heads])
        yield
        o_intra = _dot(attn, v_new)
        yield
        for h in heads:
            rows = slice(h * c, (h + 1) * c)
            o = o_intra[rows] + _dot(q_s[rows] * e_cum[rows], states[h])
            b_last = bcum[(h + 1) * c - 1:(h + 1) * c, :]
            s_ref[gi * hg + h] = (jnp.exp(b_last) * states[h]
                                  + _dot_tn(k_s[rows] * jnp.exp(b_last - bcum[rows]), v_new[rows]))
            o = o * lax.rsqrt(jnp.mean(o * o, -1, keepdims=True) + RMS_EPS) * nw_ref[...]
            o_ref[:, col(h)] = (o * _silu(g_ref[:, col(h)])).astype(o_ref.dtype)

    _interleave([group(gi) for gi in range(ng)])

    @pl.when(ci == pl.num_programs(2) - 1)
    def _():
        sout_ref[...] = s_ref[...]


def _gdn_branch(proj, small, seg, grp, s0, s_conv, conv_w, ab, norm_w):
    _, heads, dk, dv = s0.shape
    assert dk == dv
    hg = min(MIXER_HEADS_PER_STEP, heads)
    c, nc = grp.c, grp.nc
    assert c >= CONV_HALO
    conv_width = conv_w.shape[0]
    ng = min(GDN_GROUPS_PER_STEP, heads // hg)
    bw = ng * hg * dk
    r = hg * c
    qk_cols = heads * dk
    off_qkv = seg["c_qkv"][0]
    idx = np.arange(r)
    tri = jnp.asarray(((idx[:, None] // c) == (idx[None, :] // c)) & (idx[:, None] >= idx[None, :]), BF16)

    def col_spec(off):
        assert off % bw == 0
        return pl.BlockSpec((c, bw), lambda b, hi, ci: (grp.row_blk(b, ci), off // bw + hi))

    def cw_spec(j):
        return pl.BlockSpec((conv_width, bw), lambda b, hi, ci: (0, j * (qk_cols // bw) + hi))

    def cs_spec(j):
        return pl.BlockSpec((None, conv_width - 1, bw), lambda b, hi, ci: (b, 0, j * (qk_cols // bw) + hi))

    const2 = lambda a: pl.BlockSpec(a.shape, lambda b, hi, ci: (0,) * a.ndim)
    state_spec = pl.BlockSpec((None, ng * hg, dk, dv), lambda b, hi, ci: (b, hi, 0, 0))
    nw = norm_w.reshape(1, dv)
    o, s_new = pl.pallas_call(
        functools.partial(_gdn_kernel, c, hg, ng, dk, conv_width),
        grid=(grp.bsz, heads // (ng * hg), nc),
        in_specs=[col_spec(off_qkv), col_spec(off_qkv + qk_cols), col_spec(off_qkv + 2 * qk_cols),
                  col_spec(seg["c_g"][0]),
                  pl.BlockSpec((c, ng * V7X_LANES), lambda b, hi, ci: (grp.row_blk(b, ci), hi)),
                  cw_spec(0), cw_spec(1), cw_spec(2), cs_spec(0), cs_spec(1), cs_spec(2),
                  pl.BlockSpec((ng, 2, V7X_LANES), lambda b, hi, ci: (hi, 0, 0)), const2(nw), const2(tri),
                  state_spec],
        out_specs=[pl.BlockSpec((c, bw), lambda b, hi, ci: (b * nc + ci, hi)), state_spec],
        out_shape=[jax.ShapeDtypeStruct((grp.bsz * grp.t, heads * dv), BF16), jax.ShapeDtypeStruct(s0.shape, F32)],
        scratch_shapes=[pltpu.VMEM((ng * hg, dk, dv), F32), pltpu.VMEM((3, CONV_HALO + c, bw), F32)],
        compiler_params=_params("parallel", "parallel", "arbitrary"),
        name="gated_delta",
    )(proj, proj, proj, proj, small, conv_w, conv_w, conv_w, s_conv, s_conv, s_conv, ab, nw, tri, s0)
    return o, s_new


def kernel(x_prompt, x_sample, state_hgrn, state_ret, state_gdn, state_conv, lower_bounds, w_in, norm_a_w, conv_w,
           a_log, dt_bias, norm_c_w, w_br_a, w_br_b, w_br_c, w_out, ln1_g, ln1_b, ln2_g, ln2_b, ffn_w1, ffn_w3,
           ffn_w2, router, moe_w1, moe_w3, moe_w2):
    bp, tp, d = x_prompt.shape
    bs, ts, _ = x_sample.shape
    depth = state_hgrn.shape[0]
    _, _, a_heads, a_dk, a_dv = state_hgrn.shape
    _, _, b_heads, b_dk, b_dv = state_ret.shape
    _, _, c_heads, c_dk, c_dv = state_gdn.shape
    conv_width = conv_w.shape[1]
    alpha = (2 * depth) ** 0.25
    n_p = bp * tp

    a_qk, a_v = a_heads * a_dk, a_heads * a_dv
    b_qk, b_v = b_heads * b_dk, b_heads * b_dv
    c_qk, c_v = c_heads * c_dk, c_heads * c_dv
    sizes = [("a_q", a_qk), ("a_f", a_qk), ("a_i", a_v), ("a_g", a_v), ("b_q", b_qk), ("b_k", b_qk), ("b_v", b_v),
             ("b_g", b_v), ("c_qkv", 2 * c_qk + c_v), ("c_a", c_heads), ("c_b", c_heads), ("c_g", c_v),
             ("gate_a", d), ("gate_b", d), ("gate_c", d)]
    src_off, off = {}, 0
    for name, size in sizes:
        src_off[name] = off
        off += size
    seg, off = {}, 0
    for name, size in sizes:
        if name in ("c_a", "c_b"):
            continue
        seg[name] = (off, size)
        off += size
    lo, hi = src_off["c_a"], src_off["c_g"]
    w_main = jnp.concatenate([w_in[:, :, :lo], w_in[:, :, hi:]], axis=-1).astype(BF16)
    hg_c = min(MIXER_HEADS_PER_STEP, c_heads)
    n_grp = c_heads // hg_c

    def lane_groups(z):
        return z.reshape(z.shape[:-1] + (n_grp, hg_c))

    lane_pad = lambda z, used: jnp.pad(z, [(0, 0)] * (z.ndim - 1) + [(0, V7X_LANES - used)])
    w_small = lane_pad(jnp.concatenate([lane_groups(w_in[:, :, lo:lo + c_heads]),
                                        lane_groups(w_in[:, :, lo + c_heads:hi])], -1), 2 * hg_c)
    w_small = w_small.reshape(depth, d, n_grp * V7X_LANES).astype(BF16)
    ab = jnp.stack([lane_pad(lane_groups(a_log.astype(F32)), hg_c), lane_pad(lane_groups(dt_bias.astype(F32)), hg_c)],
                   axis=2)

    p = jax.nn.softmax(lower_bounds.astype(F32), axis=0)
    lb_all = jnp.cumsum(p, axis=0) - p[0:1]
    lbh = jnp.maximum(lb_all, 0.0).reshape(depth, a_heads, a_dk)
    lb3 = jnp.stack([jnp.log(lbh), jnp.log1p(-lbh), 1.0 - lbh], axis=2)

    x = jnp.concatenate([x_prompt.reshape(n_p, d), x_sample.reshape(bs * ts, d)], axis=0)
    x16 = x.astype(BF16)
    zeros = lambda b, s: jnp.zeros((b,) + s.shape[2:], F32)
    groups = (_Group(0, bp, tp), _Group(n_p, bs, ts))
    assert min(tp, ts) >= conv_width - 1
    off_qkv, n_qkv = seg["c_qkv"]
    new_states = [[] for _ in range(8)]
    for l in range(depth):
        proj = _matmul(x16, w_main[l], F32, name="in_proj")
        small = _matmul(x16, w_small[l], F32, name="in_proj_small")
        outs = []
        for gi, grp in enumerate(groups):
            if gi == 0:
                s_a, s_b, s_c, s_cv = (zeros(bp, s) for s in (state_hgrn, state_ret, state_gdn, state_conv))
                pos0 = 0
            else:
                s_a, s_b, s_c, s_cv, pos0 = state_hgrn[l], state_ret[l], state_gdn[l], state_conv[l], PAST_LEN
            o_a, s_a = _gla_branch(proj, seg, grp, s_a, lb3[l], norm_a_w[l])
            o_b, s_b = _ret_branch(proj, seg, grp, s_b, pos0)
            o_c, s_c = _gdn_branch(proj, small, seg, grp, s_c, s_cv, conv_w[l], ab[l], norm_c_w[l])
            last = (grp.row0 + jnp.arange(grp.bsz, dtype=jnp.int32)[:, None] * grp.t
                    + jnp.arange(grp.t - (conv_width - 1), grp.t, dtype=jnp.int32)[None, :]).reshape(-1)
            s_cv = jnp.take(proj, last, axis=0)[:, off_qkv:off_qkv + n_qkv].reshape(grp.bsz, conv_width - 1, n_qkv)
            outs.append((o_a, o_b, o_c))
            for i, s in enumerate((s_a, s_b, s_c, s_cv)):
                new_states[4 * gi + i].append(s)
        o_a, o_b, o_c = (jnp.concatenate([outs[0][i], outs[1][i]], axis=0) for i in range(3))
        merged = _merge(o_a, o_b, o_c, w_br_a[l].astype(BF16), w_br_b[l].astype(BF16), w_br_c[l].astype(BF16),
                        proj, [seg[n][0] for n in ("gate_a", "gate_b", "gate_c")])
        y = _matmul(merged, w_out[l].astype(BF16), F32, name="out_proj")
        x, x16 = _add_ln(x, [y], ln1_g[l], ln1_b[l], alpha)
        if l % 2 == 0:
            j = l // 2
            f = [_dense_ffn(x16, ffn_w1[j:j + 1].astype(BF16), ffn_w3[j:j + 1].astype(BF16),
                            ffn_w2[j:j + 1].astype(BF16))]
        else:
            j = l // 2
            f = _moe_ffn(x, x16, router[j], moe_w1[j].astype(BF16), moe_w3[j].astype(BF16), moe_w2[j].astype(BF16))
        x, x16 = _add_ln(x, f, ln2_g[l], ln2_b[l], alpha)

    y_prompt = x[:n_p].reshape(bp, tp, d)
    y_sample = x[n_p:].reshape(bs, ts, d)
    return (y_prompt, y_sample) + tuple(jnp.stack(s) for s in new_states)
```

```python
import functools
import math

import jax
import jax.numpy as jnp
import numpy as np
from jax import lax
from jax.experimental import pallas as pl
from jax.experimental.pallas import tpu as pltpu

F32 = jnp.float32
BF16 = jnp.bfloat16

CHUNK = 64
TOP_K = 2
PAST_LEN = 2048
ROPE_BASE = 10000.0
LN_EPS = 1e-5
RMS_EPS = 1e-6

V7X_LANES = 128
V7X_VMEM_LIMIT_BYTES = 56 * 1024 * 1024


def _tile(n, target, align=V7X_LANES):
    if n <= target:
        return n
    t = (target // align) * align
    while t >= align:
        if n % t == 0:
            return t
        t -= align
    raise ValueError(f"no {align}-aligned tile <= {target} divides {n}")


def _params(*sem):
    return pltpu.CompilerParams(dimension_semantics=sem, vmem_limit_bytes=V7X_VMEM_LIMIT_BYTES)


def _mm_kernel(x_ref, w_ref, o_ref):
    o_ref[...] = jnp.dot(x_ref[...], w_ref[...], preferred_element_type=F32).astype(o_ref.dtype)


def _matmul(x, w, out_dtype, *, bm=1024, bn=1024, name="matmul"):
    m, k = x.shape
    n = w.shape[1]
    bm, bn = _tile(m, bm, 8), _tile(n, bn)
    return pl.pallas_call(
        _mm_kernel,
        grid=(m // bm, n // bn),
        in_specs=[pl.BlockSpec((bm, k), lambda i, j: (i, 0)), pl.BlockSpec((k, bn), lambda i, j: (0, j))],
        out_specs=pl.BlockSpec((bm, bn), lambda i, j: (i, j)),
        out_shape=jax.ShapeDtypeStruct((m, n), out_dtype),
        compiler_params=_params("parallel", "parallel"),
        name=name,
    )(x, w)


def _merge_kernel(oa_ref, ob_ref, oc_ref, wa_ref, wb_ref, wc_ref, ga_ref, gb_ref, gc_ref, o_ref):
    acc = jax.nn.sigmoid(ga_ref[...]) * jnp.dot(oa_ref[...], wa_ref[...], preferred_element_type=F32)
    acc = acc + jax.nn.sigmoid(gb_ref[...]) * jnp.dot(ob_ref[...], wb_ref[...], preferred_element_type=F32)
    acc = acc + jax.nn.sigmoid(gc_ref[...]) * jnp.dot(oc_ref[...], wc_ref[...], preferred_element_type=F32)
    o_ref[...] = acc.astype(o_ref.dtype)


def _merge(o_a, o_b, o_c, w_a, w_b, w_c, proj, gate_offs, *, bm=512, bn=1024):
    m = o_a.shape[0]
    d = w_a.shape[1]
    bm, bn = _tile(m, bm, 8), _tile(d, bn)
    assert all(off % bn == 0 for off in gate_offs)
    ga, gb, gc = (off // bn for off in gate_offs)

    def o_spec(o):
        return pl.BlockSpec((bm, o.shape[1]), lambda i, j: (i, 0))

    def w_spec(w):
        return pl.BlockSpec((w.shape[0], bn), lambda i, j: (0, j))

    def g_spec(blk):
        return pl.BlockSpec((bm, bn), lambda i, j: (i, blk + j))

    return pl.pallas_call(
        _merge_kernel,
        grid=(m // bm, d // bn),
        in_specs=[o_spec(o_a), o_spec(o_b), o_spec(o_c), w_spec(w_a), w_spec(w_b), w_spec(w_c),
                  g_spec(ga), g_spec(gb), g_spec(gc)],
        out_specs=pl.BlockSpec((bm, bn), lambda i, j: (i, j)),
        out_shape=jax.ShapeDtypeStruct((m, d), BF16),
        compiler_params=_params("parallel", "parallel"),
        name="merge",
    )(o_a, o_b, o_c, w_a, w_b, w_c, proj, proj, proj)


def _add_ln_kernel(alpha, n_y, res_ref, *refs):
    y_refs, (g_ref, b_ref, o32_ref, o16_ref) = refs[:n_y], refs[n_y:]
    x = alpha * res_ref[...]
    y = y_refs[0][...]
    for r in y_refs[1:]:
        y = y + r[...]
    x = x + y
    mu = jnp.mean(x, -1, keepdims=True)
    xc = x - mu
    var = jnp.mean(xc * xc, -1, keepdims=True)
    out = xc * lax.rsqrt(var + LN_EPS) * g_ref[...] + b_ref[...]
    o32_ref[...] = out
    o16_ref[...] = out.astype(BF16)


def _add_ln(res, ys, g, b, alpha, *, bm=256):
    m, d = res.shape
    bm = _tile(m, bm, 8)
    row = pl.BlockSpec((bm, d), lambda i: (i, 0))
    vec = pl.BlockSpec((1, d), lambda i: (0, 0))
    return pl.pallas_call(
        functools.partial(_add_ln_kernel, alpha, len(ys)),
        grid=(m // bm,),
        in_specs=[row] * (1 + len(ys)) + [vec, vec],
        out_specs=[row, row],
        out_shape=[jax.ShapeDtypeStruct((m, d), F32), jax.ShapeDtypeStruct((m, d), BF16)],
        compiler_params=_params("parallel"),
        name="add_ln",
    )(res, *ys, g.reshape(1, d), b.reshape(1, d))


def _swiglu_up_kernel(has_gate, te_ref, nu_ref, x_ref, w1_ref, w3_ref, *refs):
    o_ref = refs[-1]

    @pl.when(pl.program_id(1) < nu_ref[0])
    def _():
        x = x_ref[...]
        a = jnp.dot(x, w1_ref[...], preferred_element_type=F32)
        b = jnp.dot(x, w3_ref[...], preferred_element_type=F32)
        hid = (a * jax.nn.sigmoid(a)) * b
        if has_gate:
            hid = hid * refs[0][...]
        o_ref[...] = hid.astype(o_ref.dtype)

    @pl.when(pl.program_id(1) >= nu_ref[0])
    def _():
        o_ref[...] = jnp.zeros_like(o_ref)


def _swiglu_up(x, w1, w3, row_gate, tile_expert, n_used, *, bm, bn=1024):
    m, k = x.shape
    n = w1.shape[-1]
    bn = _tile(n, bn)
    assert m % bm == 0

    def row_blk(j, i, te, nu):
        return jnp.minimum(i, nu[0] - 1)

    in_specs = [pl.BlockSpec((bm, k), lambda j, i, te, nu: (row_blk(j, i, te, nu), 0)),
                pl.BlockSpec((None, k, bn), lambda j, i, te, nu: (te[i], 0, j)),
                pl.BlockSpec((None, k, bn), lambda j, i, te, nu: (te[i], 0, j))]
    args = [x, w1, w3]
    if row_gate is not None:
        in_specs.append(pl.BlockSpec((bm, 1), lambda j, i, te, nu: (row_blk(j, i, te, nu), 0)))
        args.append(row_gate)
    return pl.pallas_call(
        functools.partial(_swiglu_up_kernel, row_gate is not None),
        grid_spec=pltpu.PrefetchScalarGridSpec(
            num_scalar_prefetch=2,
            grid=(n // bn, m // bm),
            in_specs=in_specs,
            out_specs=pl.BlockSpec((bm, bn), lambda j, i, te, nu: (i, j)),
        ),
        out_shape=jax.ShapeDtypeStruct((m, n), BF16),
        compiler_params=_params("arbitrary", "arbitrary"),
        name="swiglu_up",
    )(tile_expert, n_used, *args)


def _mm_acc_kernel(te_ref, nu_ref, x_ref, w_ref, o_ref, acc_ref):
    k = pl.program_id(2)

    @pl.when(pl.program_id(1) < nu_ref[0])
    def _():
        @pl.when(k == 0)
        def _():
            acc_ref[...] = jnp.zeros_like(acc_ref)

        acc_ref[...] += jnp.dot(x_ref[...], w_ref[...], preferred_element_type=F32)

        @pl.when(k == pl.num_programs(2) - 1)
        def _():
            o_ref[...] = acc_ref[...].astype(o_ref.dtype)

    @pl.when(pl.program_id(1) >= nu_ref[0])
    def _():
        o_ref[...] = jnp.zeros_like(o_ref)


def _grouped_matmul(x, w, tile_expert, n_used, out_dtype, *, bm, bn=2048, bk=2048):
    m, kdim = x.shape
    n = w.shape[-1]
    bn, bk = _tile(n, bn), _tile(kdim, bk)
    assert m % bm == 0

    def row_blk(i, nu):
        return jnp.minimum(i, nu[0] - 1)

    return pl.pallas_call(
        _mm_acc_kernel,
        grid_spec=pltpu.PrefetchScalarGridSpec(
            num_scalar_prefetch=2,
            grid=(n // bn, m // bm, kdim // bk),
            in_specs=[pl.BlockSpec((bm, bk), lambda j, i, kk, te, nu: (row_blk(i, nu), kk)),
                      pl.BlockSpec((None, bk, bn), lambda j, i, kk, te, nu: (te[i], kk, j))],
            out_specs=pl.BlockSpec((bm, bn), lambda j, i, kk, te, nu: (i, j)),
            scratch_shapes=[pltpu.VMEM((bm, bn), F32)],
        ),
        out_shape=jax.ShapeDtypeStruct((m, n), out_dtype),
        compiler_params=_params("arbitrary", "arbitrary", "arbitrary"),
        name="grouped_matmul",
    )(tile_expert, n_used, x, w)


def _router_kernel(n_experts, x_ref, w_ref, idx_ref, gate_ref):
    x = x_ref[...]
    w = w_ref[...]
    x_hi = x.astype(BF16)
    x_lo = (x - x_hi.astype(F32)).astype(BF16)
    w_hi = w.astype(BF16)
    w_lo = (w - w_hi.astype(F32)).astype(BF16)
    logits = (jnp.dot(x_hi, w_lo, preferred_element_type=F32) + jnp.dot(x_lo, w_hi, preferred_element_type=F32)
              + jnp.dot(x_hi, w_hi, preferred_element_type=F32))
    lane = lax.broadcasted_iota(jnp.int32, logits.shape, 1)
    neg = jnp.float32(-jnp.inf)
    l1 = jnp.where(lane < n_experts, logits, neg)
    m1 = jnp.max(l1, -1, keepdims=True)
    i1 = jnp.min(jnp.where(l1 == m1, lane, V7X_LANES), -1, keepdims=True)
    l2 = jnp.where(lane == i1, neg, l1)
    m2 = jnp.max(l2, -1, keepdims=True)
    i2 = jnp.min(jnp.where(l2 == m2, lane, V7X_LANES), -1, keepdims=True)
    e2 = jnp.exp(m2 - m1)
    den = 1.0 + e2
    idx_ref[...] = jnp.where(lane == 0, i1, jnp.where(lane == 1, i2, 0))
    gate_ref[...] = jnp.where(lane == 0, 1.0 / den, jnp.where(lane == 1, e2 / den, 0.0))


def _router(x, w_router, *, bm=512):
    m, d = x.shape
    n_experts = w_router.shape[1]
    bm = _tile(m, bm, 8)
    w_pad = jnp.pad(w_router, ((0, 0), (0, V7X_LANES - n_experts)))
    blk = pl.BlockSpec((bm, V7X_LANES), lambda i: (i, 0))
    idx, gate = pl.pallas_call(
        functools.partial(_router_kernel, n_experts),
        grid=(m // bm,),
        in_specs=[pl.BlockSpec((bm, d), lambda i: (i, 0)), pl.BlockSpec((d, V7X_LANES), lambda i: (0, 0))],
        out_specs=[blk, blk],
        out_shape=[jax.ShapeDtypeStruct((m, V7X_LANES), jnp.int32), jax.ShapeDtypeStruct((m, V7X_LANES), F32)],
        compiler_params=_params("parallel"),
        name="router",
    )(x, w_pad)
    return idx[:, :TOP_K], gate[:, :TOP_K]


def _gather_kernel(bm, idx_ref, src_ref, o_ref, buf_ref, sem):
    base = pl.program_id(0) * bm

    def row_copy(r, src_row):
        return pltpu.make_async_copy(src_ref.at[pl.ds(src_row, 1)], buf_ref.at[pl.ds(r, 1)], sem)

    def start(r, carry):
        row_copy(r, idx_ref[base + r]).start()
        return carry

    def wait(r, carry):
        row_copy(r, 0).wait()
        return carry

    lax.fori_loop(0, bm, start, 0)
    lax.fori_loop(0, bm, wait, 0)
    o_ref[...] = buf_ref[...].astype(o_ref.dtype)


def _gather_rows(src, idx, out_dtype, *, bm=256):
    n = idx.shape[0]
    d = src.shape[1]
    bm = _tile(n, bm, 8)
    return pl.pallas_call(
        functools.partial(_gather_kernel, bm),
        grid_spec=pltpu.PrefetchScalarGridSpec(
            num_scalar_prefetch=1,
            grid=(n // bm,),
            in_specs=[pl.BlockSpec(memory_space=pl.ANY)],
            out_specs=pl.BlockSpec((bm, d), lambda i, idx: (i, 0)),
            scratch_shapes=[pltpu.VMEM((bm, d), src.dtype), pltpu.SemaphoreType.DMA(())],
        ),
        out_shape=jax.ShapeDtypeStruct((n, d), out_dtype),
        compiler_params=_params("arbitrary"),
        name="gather_rows",
    )(idx, src)


def _dense_ffn(x16, w1, w3, w2, *, bm=512):
    m = x16.shape[0]
    bm = _tile(m, bm, 8)
    tile_expert = jnp.zeros((m // bm,), jnp.int32)
    n_used = jnp.full((1,), m // bm, jnp.int32)
    hid = _swiglu_up(x16, w1, w3, None, tile_expert, n_used, bm=bm)
    return _grouped_matmul(hid, w2, tile_expert, n_used, F32, bm=bm)


def _moe_ffn(x32, x16, w_router, w1, w3, w2, *, bm=512):
    t = x32.shape[0]
    n_experts = w_router.shape[1]
    bm = _tile(t, bm, 8)
    top_idx, top_w = _router(x32, w_router)
    flat_e = top_idx.reshape(-1)
    onehot = (flat_e[:, None] == jnp.arange(n_experts, dtype=jnp.int32)[None, :]).astype(jnp.int32)
    rank = jnp.sum((jnp.cumsum(onehot, axis=0) - onehot) * onehot, axis=1)
    counts = jnp.sum(onehot, axis=0)
    padded = ((counts + bm - 1) // bm) * bm
    ends = jnp.cumsum(padded)
    pos = (ends - padded)[flat_e] + rank
    n_rows = TOP_K * t + n_experts * bm
    n_tiles = n_rows // bm
    n_used = (ends[-1] // bm).astype(jnp.int32).reshape(1)
    tile_start = jnp.minimum(jnp.arange(n_tiles, dtype=jnp.int32), n_used[0] - 1) * bm
    tile_expert = jnp.minimum(jnp.searchsorted(ends, tile_start, side="right"), n_experts - 1).astype(jnp.int32)
    token_of_row = jnp.arange(TOP_K * t, dtype=jnp.int32) // TOP_K
    slot_token = jnp.zeros((n_rows,), jnp.int32).at[pos].set(token_of_row)
    slot_gate = jnp.zeros((n_rows,), F32).at[pos].set(top_w.reshape(-1))

    x_sorted = _gather_rows(x32, slot_token, BF16)
    hid = _swiglu_up(x_sorted, w1, w3, slot_gate.reshape(n_rows, 1), tile_expert, n_used, bm=bm)
    out_sorted = _grouped_matmul(hid, w2, tile_expert, n_used, F32, bm=bm)
    pos2 = pos.reshape(t, TOP_K).astype(jnp.int32)
    return [_gather_rows(out_sorted, pos2[:, k], F32) for k in range(TOP_K)]


MIXER_HEADS_PER_STEP = 4
GLA_HEADS_PER_STEP = 8
RET_HEADS_PER_STEP = 8


def _silu(x):
    return x * jax.nn.sigmoid(x)


def _dot(a, b):
    return jnp.dot(a.astype(BF16), b.astype(BF16), preferred_element_type=F32)


def _dot_nt(a, b):
    return lax.dot_general(a.astype(BF16), b.astype(BF16), (((1,), (1,)), ((), ())), preferred_element_type=F32)


def _dot_tn(a, b):
    return lax.dot_general(a.astype(BF16), b.astype(BF16), (((0,), (0,)), ((), ())), preferred_element_type=F32)


def _split2(x):
    hi = x.astype(BF16)
    lo = (x - hi.astype(F32)).astype(BF16)
    return hi, lo


def _softplus(x):
    return jnp.maximum(x, 0.0) + jnp.log1p(jnp.exp(-jnp.abs(x)))


def _interleave(gens):
    while gens:
        alive = []
        for gen in gens:
            try:
                next(gen)
                alive.append(gen)
            except StopIteration:
                pass
        gens = alive


class _Group:
    def __init__(self, row0, bsz, t):
        self.row0, self.bsz, self.t = row0, bsz, t
        self.c = math.gcd(t, CHUNK)
        self.nc = t // self.c
        assert row0 % self.c == 0
        self.blk0 = row0 // self.c

    def row_blk(self, b, ci):
        return self.blk0 + b * self.nc + ci


def _gla_consts(c):
    t = np.arange(c)[:, None]
    u = np.arange(c)[None, :]
    mats, masks, levels = [u <= t], [], []
    m = c // 2
    while m >= 1:
        same = (t // (2 * m)) == (u // (2 * m))
        t2, u2 = (t % (2 * m)) >= m, (u % (2 * m)) >= m
        mats.append(same & ((t2 & u2 & (u <= t)) | (~t2 & ~u2 & (u > t))))
        masks.append(same & t2 & ~u2)
        levels.append(m)
        m //= 2
    mats.append(u > t)
    masks.append(t == u)
    return (jnp.asarray(np.concatenate(mats, 0), BF16), jnp.asarray(np.stack(masks), F32), tuple(levels))


def _gla_kernel(c, levels, hg, dk, q_ref, f_ref, v_ref, g_ref, lb_ref, nw_ref, w_ref, mask_ref, s0_ref,
                o_ref, sout_ref, st_ref):
    ci = pl.program_id(2)

    @pl.when(ci == 0)
    def _():
        for h in range(hg):
            st_ref[h] = s0_ref[h].T

    n_lv = len(levels)
    row = lax.broadcasted_iota(jnp.int32, (c, dk), 0)
    w_all = w_ref[...]

    def head(h):
        sl = slice(h * dk, (h + 1) * dk)
        zq, zf, v, zg = q_ref[:, sl], f_ref[:, sl], v_ref[:, sl], g_ref[:, sl]
        log_lb, log1m_lb, one_m_lb = lb_ref[h, 0:1, :], lb_ref[h, 1:2, :], lb_ref[h, 2:3, :]
        q = _silu(zq)
        log_sig = jnp.minimum(zf, 0.0) - jnp.log1p(jnp.exp(-jnp.abs(zf)))
        b = log1m_lb + log_sig
        g = jnp.maximum(log_lb, b) + jnp.log1p(jnp.exp(-jnp.abs(log_lb - b)))
        k = one_m_lb * jax.nn.sigmoid(-zf)
        g_hi, g_lo = _split2(g)
        x_all = jnp.dot(w_all, g_hi, preferred_element_type=F32) + jnp.dot(w_all, g_lo, preferred_element_type=F32)
        yield
        e_all = jnp.exp(x_all)
        e_cum, e_tail = e_all[0:c], e_all[(n_lv + 1) * c:(n_lv + 2) * c]
        prods = [_dot_nt(q, k)]
        for li, m in enumerate(levels):
            e_l = e_all[(1 + li) * c:(2 + li) * c]
            second = (row & m) != 0
            prods.append(_dot_nt(jnp.where(second, q * e_l, 0.0), jnp.where(second, 0.0, k * e_l)))
        yield
        scores = mask_ref[n_lv] * prods[0]
        for li in range(n_lv):
            scores = scores + mask_ref[li] * prods[1 + li]
        st = st_ref[h]
        o = _dot(scores, v) + _dot_nt(q * e_cum, st)
        st_new = st * e_cum[c - 1:c, :] + _dot_tn(v, k * e_tail)
        yield
        st_ref[h] = st_new
        o = o * lax.rsqrt(jnp.mean(o * o, -1, keepdims=True) + RMS_EPS) * nw_ref[...]
        o_ref[:, sl] = (o * _silu(zg)).astype(o_ref.dtype)

    _interleave([head(h) for h in range(hg)])

    @pl.when(ci == pl.num_programs(2) - 1)
    def _():
        for h in range(hg):
            sout_ref[h] = st_ref[h].T


def _gla_branch(proj, seg, grp, s0, lb3, norm_w):
    _, heads, dk, dv = s0.shape
    assert dk == dv
    hg = min(GLA_HEADS_PER_STEP, heads)
    c, nc = grp.c, grp.nc
    w_all, masks, levels = _gla_consts(c)
    bw = hg * dk

    def col_spec(name):
        off = seg[name][0]
        assert off % bw == 0
        return pl.BlockSpec((c, bw), lambda b, hi, ci: (grp.row_blk(b, ci), off // bw + hi))

    const2 = lambda a: pl.BlockSpec(a.shape, lambda b, hi, ci: (0,) * a.ndim)
    state_spec = pl.BlockSpec((None, hg, dk, dv), lambda b, hi, ci: (b, hi, 0, 0))
    nw = norm_w.reshape(1, dv)
    o, s_new = pl.pallas_call(
        functools.partial(_gla_kernel, c, levels, hg, dk),
        grid=(grp.bsz, heads // hg, nc),
        in_specs=[col_spec("a_q"), col_spec("a_f"), col_spec("a_i"), col_spec("a_g"),
                  pl.BlockSpec((hg, 3, dk), lambda b, hi, ci: (hi, 0, 0)), const2(nw), const2(w_all), const2(masks),
                  state_spec],
        out_specs=[pl.BlockSpec((c, bw), lambda b, hi, ci: (b * nc + ci, hi)), state_spec],
        out_shape=[jax.ShapeDtypeStruct((grp.bsz * grp.t, heads * dv), BF16), jax.ShapeDtypeStruct(s0.shape, F32)],
        scratch_shapes=[pltpu.VMEM((hg, dv, dk), F32)],
        compiler_params=_params("parallel", "parallel", "arbitrary"),
        name="hgrn2",
    )(proj, proj, proj, proj, lb3, nw, w_all, masks, s0)
    return o, s_new


def _ret_kernel(c, hg, dk, dv, q_ref, k_ref, v_ref, g_ref, cos_ref, sin_ref, intra_ref, inner_ref, tail_ref,
                whole_ref, s0_ref, o_ref, sout_ref, s_ref):
    ci = pl.program_id(2)

    @pl.when(ci == 0)
    def _():
        s_ref[...] = s0_ref[...]

    cos, sin = cos_ref[...], sin_ref[...]
    def head(h):
        sk, sv = slice(h * dk, (h + 1) * dk), slice(h * dv, (h + 1) * dv)
        zq, zk, v, zg = q_ref[:, sk], k_ref[:, sk], v_ref[:, sv], g_ref[:, sv]
        q = zq * cos + pltpu.roll(zq, dk // 2, 1) * sin
        k = (zk * cos + pltpu.roll(zk, dk // 2, 1) * sin) * (dk ** -0.5)
        s = s_ref[h]
        qk = _dot_nt(q, k)
        inter = _dot(q, s)
        s_new = whole_ref[h] * s + _dot_tn(k * tail_ref[h], v)
        yield
        o = _dot(qk * intra_ref[h], v) + inter * inner_ref[h]
        yield
        s_ref[h] = s_new
        mu = jnp.mean(o, -1, keepdims=True)
        oc = o - mu
        var = jnp.mean(oc * oc, -1, keepdims=True)
        o_ref[:, sv] = (oc * lax.rsqrt(var + LN_EPS) * _silu(zg)).astype(o_ref.dtype)

    _interleave([head(h) for h in range(hg)])

    @pl.when(ci == pl.num_programs(2) - 1)
    def _():
        sout_ref[...] = s_ref[...]


def _ret_branch(proj, seg, grp, s0, pos0):
    _, heads, dk, dv = s0.shape
    hg = min(RET_HEADS_PER_STEP, heads)
    c, nc = grp.c, grp.nc
    half = dk // 2
    inv_freq = ROPE_BASE ** (-jnp.arange(half, dtype=F32) / half)
    ang = (pos0 + jnp.arange(grp.t)).astype(F32)[:, None] * inv_freq[None, :]
    cos = jnp.concatenate([jnp.cos(ang), jnp.cos(ang)], -1)
    sin = jnp.concatenate([-jnp.sin(ang), jnp.sin(ang)], -1)
    log_gamma = jnp.log1p(-jnp.exp2(-5.0 - jnp.arange(heads, dtype=F32)))
    idx = jnp.arange(c, dtype=F32)
    rel = idx[:, None] - idx[None, :]
    intra = jnp.where(rel >= 0, jnp.exp(jnp.maximum(rel, 0.0) * log_gamma[:, None, None]), 0.0)
    inner = jnp.broadcast_to(jnp.exp((idx + 1.0)[None, :] * log_gamma[:, None])[:, :, None], (heads, c, dv))
    tail = jnp.broadcast_to(jnp.exp((c - 1.0 - idx)[None, :] * log_gamma[:, None])[:, :, None], (heads, c, dk))
    whole = jnp.broadcast_to(jnp.exp(c * log_gamma)[:, None, None], (heads, 1, dv))

    def col_spec(name, width):
        off = seg[name][0]
        bw = hg * width
        assert off % bw == 0
        return pl.BlockSpec((c, bw), lambda b, hi, ci: (grp.row_blk(b, ci), off // bw + hi))

    tab_spec = pl.BlockSpec((c, dk), lambda b, hi, ci: (ci, 0))
    head_spec = lambda a: pl.BlockSpec((hg,) + a.shape[1:], lambda b, hi, ci: (hi, 0, 0))
    state_spec = pl.BlockSpec((None, hg, dk, dv), lambda b, hi, ci: (b, hi, 0, 0))
    o, s_new = pl.pallas_call(
        functools.partial(_ret_kernel, c, hg, dk, dv),
        grid=(grp.bsz, heads // hg, nc),
        in_specs=[col_spec("b_q", dk), col_spec("b_k", dk), col_spec("b_v", dv), col_spec("b_g", dv), tab_spec,
                  tab_spec, head_spec(intra), head_spec(inner), head_spec(tail), head_spec(whole), state_spec],
        out_specs=[pl.BlockSpec((c, hg * dv), lambda b, hi, ci: (b * nc + ci, hi)), state_spec],
        out_shape=[jax.ShapeDtypeStruct((grp.bsz * grp.t, heads * dv), BF16), jax.ShapeDtypeStruct(s0.shape, F32)],
        scratch_shapes=[pltpu.VMEM((hg, dk, dv), F32)],
        compiler_params=_params("parallel", "parallel", "arbitrary"),
        name="retention",
    )(proj, proj, proj, proj, cos, sin, intra, inner, tail, whole, s0)
    return o, s_new


CONV_HALO = 8
GDN_GROUPS_PER_STEP = 2


def _gdn_kernel(c, hg, ng, dk, conv_width, q_ref, k_ref, v_ref, g_ref, small_ref, cwq_ref, cwk_ref, cwv_ref,
                csq_ref, csk_ref, csv_ref, ab_ref, nw_ref, tri_ref, s0_ref, o_ref, sout_ref, s_ref, ext_ref):
    ci = pl.program_id(2)
    r = hg * c

    @pl.when(ci == 0)
    def _():
        s_ref[...] = s0_ref[...]
        ext_ref[...] = jnp.zeros_like(ext_ref)
        for j, cs in enumerate((csq_ref, csk_ref, csv_ref)):
            ext_ref[j, CONV_HALO - (conv_width - 1):CONV_HALO, :] = cs[...]

    conv = []
    for j, (x_ref, cw_ref) in enumerate(((q_ref, cwq_ref), (k_ref, cwk_ref), (v_ref, cwv_ref))):
        x = x_ref[...]
        ext_ref[j, CONV_HALO:CONV_HALO + c, :] = x
        acc = x * cw_ref[conv_width - 1:conv_width, :]
        for w in range(conv_width - 1):
            shift = conv_width - 1 - w
            acc = acc + ext_ref[j, CONV_HALO - shift:CONV_HALO - shift + c, :] * cw_ref[w:w + 1, :]
        ext_ref[j, 0:CONV_HALO, :] = x[c - CONV_HALO:c, :]
        conv.append(_silu(acc))

    def l2n(x):
        return x * lax.rsqrt(jnp.sum(x * x, -1, keepdims=True) + RMS_EPS)

    def stack(parts):
        return jnp.concatenate(parts, axis=0)

    tri = tri_ref[...]
    lane = lax.broadcasted_iota(jnp.int32, (r, dk), 1)
    rr = lax.broadcasted_iota(jnp.int32, (r, r), 0)
    cc = lax.broadcasted_iota(jnp.int32, (r, r), 1)
    same = (rr // c) == (cc // c)
    incl = same & (rr >= cc)
    strict = same & (rr > cc)
    n_sq = int(round(math.log2(c))) - 1
    heads = range(hg)

    def group(gi):
        col = lambda h: slice((gi * hg + h) * dk, (gi * hg + h + 1) * dk)
        q_s = stack([l2n(conv[0][:, col(h)]) * (dk ** -0.5) for h in heads])
        k_s = stack([l2n(conv[1][:, col(h)]) for h in heads])
        v_s = stack([conv[2][:, col(h)] for h in heads])
        small = small_ref[:, gi * V7X_LANES:(gi + 1) * V7X_LANES]
        g_all = -jnp.exp(ab_ref[gi, 0:1, :]) * _softplus(small + ab_ref[gi, 1:2, :])
        beta_all = jax.nn.sigmoid(small)
        g_b = stack([jnp.broadcast_to(g_all[:, h:h + 1], (c, dk)) for h in heads])
        beta_b = stack([jnp.broadcast_to(beta_all[:, hg + h:hg + h + 1], (c, dk)) for h in heads])
        g_hi, g_lo = _split2(g_b)
        bcum = jnp.dot(tri, g_hi, preferred_element_type=F32) + jnp.dot(tri, g_lo, preferred_element_type=F32)
        kb = k_s * beta_b
        gram = _dot_nt(stack([kb, q_s]), k_s)
        yield
        b_hi = bcum.astype(BF16).astype(F32)
        b_lo = (bcum - b_hi).astype(BF16).astype(F32)
        lhs = jnp.where(lane == 0, b_hi, jnp.where(lane == 1, b_lo, jnp.where(lane < 4, 1.0, 0.0)))
        rhs = jnp.where(lane < 2, 1.0, jnp.where(lane == 2, -b_hi, jnp.where(lane == 3, -b_lo, 0.0)))
        diff = _dot_nt(lhs, rhs)
        yield
        lmask = jnp.where(incl, jnp.exp(jnp.where(incl, diff, 0.0)), 0.0)
        a_low = jnp.where(strict, gram[0:r] * lmask, 0.0)
        attn = gram[r:2 * r] * lmask
        y = -a_low
        p = _dot(a_low, a_low)
        yield
        for j in range(n_sq):
            yp = _dot(y, p)
            p_next = _dot(p, p) if j + 1 < n_sq else None
            yield
            y = y + p + yp
            p = p_next
        e_cum = jnp.exp(bcum)
        rhs_all = jnp.concatenate([v_s * beta_b, kb * e_cum], axis=1)
        rhs_hi, rhs_lo = _split2(rhs_all)
        y16 = y.astype(BF16)
        sol = (rhs_all + jnp.dot(y16, rhs_hi, preferred_element_type=F32)
               + jnp.dot(y16, rhs_lo, preferred_element_type=F32))
        yield
        u, w = sol[:, 0:dk], sol[:, dk:2 * dk]
        states = [s_ref[gi * hg + h] for h in heads]
        v_new = stack([u[h * c:(h + 1) * c] - _dot(w[h * c:(h + 1) * c], states[h]) for h in heads])
        yield
        o_intra = _dot(attn, v_new)
        yield
        for h in heads:
            rows = slice(h * c, (h + 1) * c)
            o = o_intra[rows] + _dot(q_s[rows] * e_cum[rows], states[h])
            b_last = bcum[(h + 1) * c - 1:(h + 1) * c, :]
            s_ref[gi * hg + h] = (jnp.exp(b_last) * states[h]
                                  + _dot_tn(k_s[rows] * jnp.exp(b_last - bcum[rows]), v_new[rows]))
            o = o * lax.rsqrt(jnp.mean(o * o, -1, keepdims=True) + RMS_EPS) * nw_ref[...]
            o_ref[:, col(h)] = (o * _silu(g_ref[:, col(h)])).astype(o_ref.dtype)

    _interleave([group(gi) for gi in range(ng)])

    @pl.when(ci == pl.num_programs(2) - 1)
    def _():
        sout_ref[...] = s_ref[...]


def _gdn_branch(proj, small, seg, grp, s0, s_conv, conv_w, ab, norm_w):
    _, heads, dk, dv = s0.shape
    assert dk == dv
    hg = min(MIXER_HEADS_PER_STEP, heads)
    c, nc = grp.c, grp.nc
    assert c >= CONV_HALO
    conv_width = conv_w.shape[0]
    ng = min(GDN_GROUPS_PER_STEP, heads // hg)
    bw = ng * hg * dk
    r = hg * c
    qk_cols = heads * dk
    off_qkv = seg["c_qkv"][0]
    idx = np.arange(r)
    tri = jnp.asarray(((idx[:, None] // c) == (idx[None, :] // c)) & (idx[:, None] >= idx[None, :]), BF16)

    def col_spec(off):
        assert off % bw == 0
        return pl.BlockSpec((c, bw), lambda b, hi, ci: (grp.row_blk(b, ci), off // bw + hi))

    def cw_spec(j):
        return pl.BlockSpec((conv_width, bw), lambda b, hi, ci: (0, j * (qk_cols // bw) + hi))

    def cs_spec(j):
        return pl.BlockSpec((None, conv_width - 1, bw), lambda b, hi, ci: (b, 0, j * (qk_cols // bw) + hi))

    const2 = lambda a: pl.BlockSpec(a.shape, lambda b, hi, ci: (0,) * a.ndim)
    state_spec = pl.BlockSpec((None, ng * hg, dk, dv), lambda b, hi, ci: (b, hi, 0, 0))
    nw = norm_w.reshape(1, dv)
    o, s_new = pl.pallas_call(
        functools.partial(_gdn_kernel, c, hg, ng, dk, conv_width),
        grid=(grp.bsz, heads // (ng * hg), nc),
        in_specs=[col_spec(off_qkv), col_spec(off_qkv + qk_cols), col_spec(off_qkv + 2 * qk_cols),
                  col_spec(seg["c_g"][0]),
                  pl.BlockSpec((c, ng * V7X_LANES), lambda b, hi, ci: (grp.row_blk(b, ci), hi)),
                  cw_spec(0), cw_spec(1), cw_spec(2), cs_spec(0), cs_spec(1), cs_spec(2),
                  pl.BlockSpec((ng, 2, V7X_LANES), lambda b, hi, ci: (hi, 0, 0)), const2(nw), const2(tri),
                  state_spec],
        out_specs=[pl.BlockSpec((c, bw), lambda b, hi, ci: (b * nc + ci, hi)), state_spec],
        out_shape=[jax.ShapeDtypeStruct((grp.bsz * grp.t, heads * dv), BF16), jax.ShapeDtypeStruct(s0.shape, F32)],
        scratch_shapes=[pltpu.VMEM((ng * hg, dk, dv), F32), pltpu.VMEM((3, CONV_HALO + c, bw), F32)],
        compiler_params=_params("parallel", "parallel", "arbitrary"),
        name="gated_delta",
    )(proj, proj, proj, proj, small, conv_w, conv_w, conv_w, s_conv, s_conv, s_conv, ab, nw, tri, s0)
    return o, s_new


def kernel(x_prompt, x_sample, state_hgrn, state_ret, state_gdn, state_conv, lower_bounds, w_in, norm_a_w, conv_w,
           a_log, dt_bias, norm_c_w, w_br_a, w_br_b, w_br_c, w_out, ln1_g, ln1_b, ln2_g, ln2_b, ffn_w1, ffn_w3,
           ffn_w2, router, moe_w1, moe_w3, moe_w2):
    bp, tp, d = x_prompt.shape
    bs, ts, _ = x_sample.shape
    depth = state_hgrn.shape[0]
    _, _, a_heads, a_dk, a_dv = state_hgrn.shape
    _, _, b_heads, b_dk, b_dv = state_ret.shape
    _, _, c_heads, c_dk, c_dv = state_gdn.shape
    conv_width = conv_w.shape[1]
    alpha = (2 * depth) ** 0.25
    n_p = bp * tp

    a_qk, a_v = a_heads * a_dk, a_heads * a_dv
    b_qk, b_v = b_heads * b_dk, b_heads * b_dv
    c_qk, c_v = c_heads * c_dk, c_heads * c_dv
    sizes = [("a_q", a_qk), ("a_f", a_qk), ("a_i", a_v), ("a_g", a_v), ("b_q", b_qk), ("b_k", b_qk), ("b_v", b_v),
             ("b_g", b_v), ("c_qkv", 2 * c_qk + c_v), ("c_a", c_heads), ("c_b", c_heads), ("c_g", c_v),
             ("gate_a", d), ("gate_b", d), ("gate_c", d)]
    src_off, off = {}, 0
    for name, size in sizes:
        src_off[name] = off
        off += size
    seg, off = {}, 0
    for name, size in sizes:
        if name in ("c_a", "c_b"):
            continue
        seg[name] = (off, size)
        off += size
    lo, hi = src_off["c_a"], src_off["c_g"]
    w_main = [jnp.concatenate([w_in[l, :, :lo].astype(BF16), w_in[l, :, hi:].astype(BF16)], axis=-1)
              for l in range(depth)]
    hg_c = min(MIXER_HEADS_PER_STEP, c_heads)
    n_grp = c_heads // hg_c

    def lane_groups(z):
        return z.reshape(z.shape[:-1] + (n_grp, hg_c))

    lane_pad = lambda z, used: jnp.pad(z, [(0, 0)] * (z.ndim - 1) + [(0, V7X_LANES - used)])
    w_small = lane_pad(jnp.concatenate([lane_groups(w_in[:, :, lo:lo + c_heads]),
                                        lane_groups(w_in[:, :, lo + c_heads:hi])], -1), 2 * hg_c)
    w_small = w_small.reshape(depth, d, n_grp * V7X_LANES).astype(BF16)
    ab = jnp.stack([lane_pad(lane_groups(a_log.astype(F32)), hg_c), lane_pad(lane_groups(dt_bias.astype(F32)), hg_c)],
                   axis=2)

    p = jax.nn.softmax(lower_bounds.astype(F32), axis=0)
    lb_all = jnp.cumsum(p, axis=0) - p[0:1]
    lbh = jnp.maximum(lb_all, 0.0).reshape(depth, a_heads, a_dk)
    lb3 = jnp.stack([jnp.log(lbh), jnp.log1p(-lbh), 1.0 - lbh], axis=2)

    x = jnp.concatenate([x_prompt.reshape(n_p, d), x_sample.reshape(bs * ts, d)], axis=0)
    x16 = x.astype(BF16)
    zeros = lambda b, s: jnp.zeros((b,) + s.shape[2:], F32)
    groups = (_Group(0, bp, tp), _Group(n_p, bs, ts))
    assert min(tp, ts) >= conv_width - 1
    off_qkv, n_qkv = seg["c_qkv"]
    new_states = [[] for _ in range(8)]
    for l in range(depth):
        proj = _matmul(x16, w_main[l], F32, name="in_proj")
        small = _matmul(x16, w_small[l], F32, name="in_proj_small")
        outs = []
        for gi, grp in enumerate(groups):
            if gi == 0:
                s_a, s_b, s_c, s_cv = (zeros(bp, s) for s in (state_hgrn, state_ret, state_gdn, state_conv))
                pos0 = 0
            else:
                s_a, s_b, s_c, s_cv, pos0 = state_hgrn[l], state_ret[l], state_gdn[l], state_conv[l], PAST_LEN
            o_a, s_a = _gla_branch(proj, seg, grp, s_a, lb3[l], norm_a_w[l])
            o_b, s_b = _ret_branch(proj, seg, grp, s_b, pos0)
            o_c, s_c = _gdn_branch(proj, small, seg, grp, s_c, s_cv, conv_w[l], ab[l], norm_c_w[l])
            last = (grp.row0 + jnp.arange(grp.bsz, dtype=jnp.int32)[:, None] * grp.t
                    + jnp.arange(grp.t - (conv_width - 1), grp.t, dtype=jnp.int32)[None, :]).reshape(-1)
            s_cv = jnp.take(proj, last, axis=0)[:, off_qkv:off_qkv + n_qkv].reshape(grp.bsz, conv_width - 1, n_qkv)
            outs.append((o_a, o_b, o_c))
            for i, s in enumerate((s_a, s_b, s_c, s_cv)):
                new_states[4 * gi + i].append(s)
        o_a, o_b, o_c = (jnp.concatenate([outs[0][i], outs[1][i]], axis=0) for i in range(3))
        merged = _merge(o_a, o_b, o_c, w_br_a[l].astype(BF16), w_br_b[l].astype(BF16), w_br_c[l].astype(BF16),
                        proj, [seg[n][0] for n in ("gate_a", "gate_b", "gate_c")])
        y = _matmul(merged, w_out[l].astype(BF16), F32, name="out_proj")
        x, x16 = _add_ln(x, [y], ln1_g[l], ln1_b[l], alpha)
        if l % 2 == 0:
            j = l // 2
            f = [_dense_ffn(x16, ffn_w1[j:j + 1].astype(BF16), ffn_w3[j:j + 1].astype(BF16),
                            ffn_w2[j:j + 1].astype(BF16))]
        else:
            j = l // 2
            f = _moe_ffn(x, x16, router[j], moe_w1[j].astype(BF16), moe_w3[j].astype(BF16), moe_w2[j].astype(BF16))
        x, x16 = _add_ln(x, f, ln2_g[l], ln2_b[l], alpha)

    y_prompt = x[:n_p].reshape(bp, tp, d)
    y_sample = x[n_p:].reshape(bs, ts, d)
    return (y_prompt, y_sample) + tuple(jnp.stack(s) for s in new_states)
```
